```python
import math
import jax, jax.numpy as jnp
from jax import lax
import numpy as np

D_MODEL = 1024
BATCH = 8
SEQ = 4096
DEPTH = 2

N_EVEN = (DEPTH + 1) // 2
N_ODD = DEPTH // 2

A_HEADS = 8
HEAD_DIM = 64
A_WIDTH = A_HEADS * HEAD_DIM
ATTN_BLOCK = 128
B_WIDTH = D_MODEL // 2
B_BLOCKS = 8
B_BLOCK_DIM = B_WIDTH // B_BLOCKS
CONV_WIDTH = 4
RG_C = 8.0
IN_AB = 3 * A_WIDTH + A_HEADS + 2 * B_WIDTH
C_WIDTH = 2 * D_MODEL
C_GROUPS = 8
C_GROUP_DIM = C_WIDTH // C_GROUPS
C_CHUNK = 128
MEM_LEN = 256
X_HEADS = 4
X_HEAD_DIM = D_MODEL // X_HEADS
N_GROUPS = 4
EXPERTS_PER_GROUP = 8
N_EXPERTS = N_GROUPS * EXPERTS_PER_GROUP
TOP_K = 2
D_EXPERT = D_MODEL // 2
EPS = 1e-6

kernel_name = "hybrid_fox_rglru_gmlp_hmoe"


def rms_norm(x, g):
    xf = x.astype(jnp.float32)
    y = xf * lax.rsqrt(jnp.mean(xf * xf, axis=-1, keepdims=True) + EPS)
    return (y * g.astype(jnp.float32)).astype(x.dtype)


def forgetting_attention(q, k, v, f_logit, f_bias, q_gain, k_gain):
    B, S, _ = q.shape
    q = rms_norm(q.reshape(B, S, A_HEADS, HEAD_DIM), q_gain).transpose(0, 2, 1, 3)
    k = rms_norm(k.reshape(B, S, A_HEADS, HEAD_DIM), k_gain).transpose(0, 2, 1, 3)
    v = v.reshape(B, S, A_HEADS, HEAD_DIM).transpose(0, 2, 1, 3)
    log_f = jax.nn.log_sigmoid(f_logit.astype(jnp.float32) + f_bias.astype(jnp.float32))
    c = jnp.cumsum(log_f, axis=1).transpose(0, 2, 1)
    scale = HEAD_DIM ** -0.5
    kpos = jnp.arange(S)

    def block(i):
        start = i * ATTN_BLOCK
        qb = lax.dynamic_slice_in_dim(q, start, ATTN_BLOCK, axis=2)
        cq = lax.dynamic_slice_in_dim(c, start, ATTN_BLOCK, axis=2)
        s = jnp.einsum('bhqd,bhkd->bhqk', qb, k, preferred_element_type=jnp.float32) * scale
        s = s + cq[..., :, None] - c[..., None, :]
        qpos = start + jnp.arange(ATTN_BLOCK)
        s = jnp.where(kpos[None, :] <= qpos[:, None], s, -jnp.inf)
        p = jax.nn.softmax(s, axis=-1).astype(v.dtype)
        return jnp.einsum('bhqk,bhkd->bhqd', p, v)

    o = lax.map(block, jnp.arange(S // ATTN_BLOCK))
    return o.transpose(1, 0, 3, 2, 4).reshape(B, S, A_WIDTH)


def causal_depthwise_conv(x, w, b):
    S = x.shape[1]
    xp = jnp.pad(x, ((0, 0), (CONV_WIDTH - 1, 0), (0, 0)))
    y = sum(xp[:, k:k + S] * w[k] for k in range(CONV_WIDTH))
    return y + b


def rglru_branch(xb, gb, conv_w, conv_b, wa, ba, wx, bx, lam):
    B, S, _ = xb.shape
    gate = jax.nn.gelu(gb)
    xc = causal_depthwise_conv(xb, conv_w, conv_b)
    xg = xc.reshape(B, S, B_BLOCKS, B_BLOCK_DIM)
    r = jax.nn.sigmoid(jnp.einsum('bsgi,gij->bsgj', xg, wa).reshape(B, S, B_WIDTH) + ba)
    i = jax.nn.sigmoid(jnp.einsum('bsgi,gij->bsgj', xg, wx).reshape(B, S, B_WIDTH) + bx)
    log_a = -RG_C * r.astype(jnp.float32) * jax.nn.softplus(-lam.astype(jnp.float32))
    a = jnp.exp(log_a)
    beta = jnp.sqrt(-jnp.expm1(2.0 * log_a))
    bterm = beta * (i * xc).astype(jnp.float32)

    def combine(left, right):
        a_l, b_l = left
        a_r, b_r = right
        return a_l * a_r, a_r * b_l + b_r

    _, h = lax.associative_scan(combine, (a, bterm), axis=1)
    return gate * h.astype(xb.dtype)


def fox_rglru_mixer(xn, w_in, f_bias, q_gain, k_gain, conv_w, conv_b, wa, ba, wx, bx, lam, w_out):
    proj = xn @ w_in
    cuts = [A_WIDTH, 2 * A_WIDTH, 3 * A_WIDTH, 3 * A_WIDTH + A_HEADS,
            3 * A_WIDTH + A_HEADS + B_WIDTH]
    q, k, v, f_logit, xb, gb = jnp.split(proj, cuts, axis=-1)
    attn = forgetting_attention(q, k, v, f_logit, f_bias, q_gain, k_gain)
    rec = rglru_branch(xb, gb, conv_w, conv_b, wa, ba, wx, bx, lam)
    return jnp.concatenate([attn, rec], axis=-1) @ w_out


def spatial_gating_mixer(xn, w_in, b_in, v_gain, w_s, b_s, w_out):
    B, S, _ = xn.shape
    z = jax.nn.gelu(xn @ w_in + b_in)
    u, v = jnp.split(z, 2, axis=-1)
    v = rms_norm(v, v_gain)
    v = v.reshape(B, S // C_CHUNK, C_CHUNK, C_GROUPS, C_GROUP_DIM)
    w_causal = jnp.tril(w_s)
    mixed = jnp.einsum('gts,bnsgc->bntgc', w_causal, v) + b_s.T[:, :, None]
    return (u * mixed.reshape(B, S, C_WIDTH)) @ w_out


def memory_cross_attention(xn, memn, wq, wkv, wo, q_gain, k_gain):
    B, S, _ = xn.shape
    q = rms_norm((xn @ wq).reshape(B, S, X_HEADS, X_HEAD_DIM), q_gain)
    k, v = jnp.split(memn @ wkv, 2, axis=-1)
    k = rms_norm(k.reshape(B, MEM_LEN, X_HEADS, X_HEAD_DIM), k_gain)
    v = v.reshape(B, MEM_LEN, X_HEADS, X_HEAD_DIM)
    s = jnp.einsum('bqhd,bkhd->bhqk', q, k, preferred_element_type=jnp.float32) * X_HEAD_DIM ** -0.5
    p = jax.nn.softmax(s, axis=-1).astype(v.dtype)
    o = jnp.einsum('bhqk,bkhd->bqhd', p, v).reshape(B, S, D_MODEL)
    return o @ wo


def hierarchical_moe(xn, wg, bg, we, be, w_gate, w_up, w_down):
    B, S, D = xn.shape
    T = B * S
    xf = xn.reshape(T, D)
    g_prob = jax.nn.softmax((xf @ wg + bg).astype(jnp.float32), axis=-1)
    g_idx = jnp.argmax(g_prob, axis=-1)
    g_w = jnp.take_along_axis(g_prob, g_idx[:, None], axis=-1)
    e_logits = (xf @ we + be).astype(jnp.float32).reshape(T, N_GROUPS, EXPERTS_PER_GROUP)
    e_sel = jnp.take_along_axis(e_logits, g_idx[:, None, None], axis=1)[:, 0]
    top_v, top_i = lax.top_k(e_sel, TOP_K)
    weights = (g_w * jax.nn.softmax(top_v, axis=-1)).astype(xn.dtype)
    expert = (g_idx[:, None] * EXPERTS_PER_GROUP + top_i).reshape(-1)
    order = jnp.argsort(expert)
    tok = order // TOP_K
    sizes = jnp.bincount(expert, length=N_EXPERTS).astype(jnp.int32)
    xs = xf[tok]
    h = jax.nn.silu(lax.ragged_dot(xs, w_gate, sizes)) * lax.ragged_dot(xs, w_up, sizes)
    y = lax.ragged_dot(h, w_down, sizes) * weights.reshape(-1)[order][:, None]
    out = jax.ops.segment_sum(y, tok, num_segments=T)
    return out.reshape(B, S, D)


def setup_inputs(seed: int = 0) -> dict:
    key = jax.random.key(seed)
    keys = list(jax.random.split(key, 40))
    kit = iter(keys)

    def nrm(shape, scale):
        return jax.random.normal(next(kit), shape, jnp.float32) * scale

    def gain(shape):
        return 1.0 + nrm(shape, 0.02)

    D = D_MODEL
    x = nrm((BATCH, SEQ, D), 1.0)
    mem = nrm((BATCH, MEM_LEN, D), 1.0)
    u = jax.random.uniform(next(kit), (N_EVEN, B_WIDTH), jnp.float32, 0.9, 0.999)
    sg = u ** (1.0 / RG_C)
    lam = jnp.log(sg) - jnp.log1p(-sg)
    return {
        "x": x,
        "mem": mem,
        "norm_mix": gain((DEPTH, D)),
        "norm_cross": gain((DEPTH, D)),
        "norm_mem": gain((DEPTH, D)),
        "norm_ffn": gain((DEPTH, D)),
        "ab_w_in": nrm((N_EVEN, D, IN_AB), D ** -0.5),
        "ab_f_bias": 3.0 + nrm((N_EVEN, A_HEADS), 1.0),
        "ab_q_gain": gain((N_EVEN, HEAD_DIM)),
        "ab_k_gain": gain((N_EVEN, HEAD_DIM)),
        "ab_conv_w": nrm((N_EVEN, CONV_WIDTH, B_WIDTH), CONV_WIDTH ** -0.5),
        "ab_conv_b": nrm((N_EVEN, B_WIDTH), 0.02),
        "ab_wa": nrm((N_EVEN, B_BLOCKS, B_BLOCK_DIM, B_BLOCK_DIM), B_BLOCK_DIM ** -0.5),
        "ab_ba": nrm((N_EVEN, B_WIDTH), 0.02),
        "ab_wx": nrm((N_EVEN, B_BLOCKS, B_BLOCK_DIM, B_BLOCK_DIM), B_BLOCK_DIM ** -0.5),
        "ab_bx": nrm((N_EVEN, B_WIDTH), 0.02),
        "ab_lambda": lam,
        "ab_w_out": nrm((N_EVEN, A_WIDTH + B_WIDTH, D), (A_WIDTH + B_WIDTH) ** -0.5),
        "c_w_in": nrm((N_ODD, D, 2 * C_WIDTH), D ** -0.5),
        "c_b_in": nrm((N_ODD, 2 * C_WIDTH), 0.02),
        "c_v_gain": gain((N_ODD, C_WIDTH)),
        "c_w_s": nrm((N_ODD, C_GROUPS, C_CHUNK, C_CHUNK), C_CHUNK ** -0.5),
        "c_b_s": 1.0 + nrm((N_ODD, C_GROUPS, C_CHUNK), 0.1),
        "c_w_out": nrm((N_ODD, C_WIDTH, D), C_WIDTH ** -0.5),
        "x_wq": nrm((DEPTH, D, D), D ** -0.5),
        "x_wkv": nrm((DEPTH, D, 2 * D), D ** -0.5),
        "x_wo": nrm((DEPTH, D, D), D ** -0.5),
        "x_q_gain": gain((DEPTH, X_HEAD_DIM)),
        "x_k_gain": gain((DEPTH, X_HEAD_DIM)),
        "moe_wg": nrm((DEPTH, D, N_GROUPS), D ** -0.5),
        "moe_bg": nrm((DEPTH, N_GROUPS), 0.01),
        "moe_we": nrm((DEPTH, D, N_EXPERTS), D ** -0.5),
        "moe_be": nrm((DEPTH, N_EXPERTS), 0.01),
        "moe_w_gate": nrm((DEPTH, N_EXPERTS, D, D_EXPERT), D ** -0.5),
        "moe_w_up": nrm((DEPTH, N_EXPERTS, D, D_EXPERT), D ** -0.5),
        "moe_w_down": nrm((DEPTH, N_EXPERTS, D_EXPERT, D), D_EXPERT ** -0.5),
    }


def reference(x, mem, norm_mix, norm_cross, norm_mem, norm_ffn,
              ab_w_in, ab_f_bias, ab_q_gain, ab_k_gain, ab_conv_w, ab_conv_b,
              ab_wa, ab_ba, ab_wx, ab_bx, ab_lambda, ab_w_out,
              c_w_in, c_b_in, c_v_gain, c_w_s, c_b_s, c_w_out,
              x_wq, x_wkv, x_wo, x_q_gain, x_k_gain,
              moe_wg, moe_bg, moe_we, moe_be, moe_w_gate, moe_w_up, moe_w_down):
    for layer in range(DEPTH):
        j = layer // 2
        h = rms_norm(x, norm_mix[layer])
        if layer % 2 == 0:
            x = x + fox_rglru_mixer(h, ab_w_in[j], ab_f_bias[j], ab_q_gain[j], ab_k_gain[j],
                                    ab_conv_w[j], ab_conv_b[j], ab_wa[j], ab_ba[j],
                                    ab_wx[j], ab_bx[j], ab_lambda[j], ab_w_out[j])
        else:
            x = x + spatial_gating_mixer(h, c_w_in[j], c_b_in[j], c_v_gain[j],
                                         c_w_s[j], c_b_s[j], c_w_out[j])
        x = x + memory_cross_attention(rms_norm(x, norm_cross[layer]),
                                       rms_norm(mem, norm_mem[layer]),
                                       x_wq[layer], x_wkv[layer], x_wo[layer],
                                       x_q_gain[layer], x_k_gain[layer])
        x = x + hierarchical_moe(rms_norm(x, norm_ffn[layer]), moe_wg[layer], moe_bg[layer],
                                 moe_we[layer], moe_be[layer], moe_w_gate[layer],
                                 moe_w_up[layer], moe_w_down[layer])
    return x
```

```python
import functools
import math

import jax
import jax.numpy as jnp
from jax import lax
from jax.experimental import pallas as pl
from jax.experimental.pallas import tpu as pltpu

EPS = 1e-6
A_HEADS = 8
HEAD_DIM = 64
A_WIDTH = A_HEADS * HEAD_DIM
B_WIDTH = 512
B_BLOCKS = 8
CONV_WIDTH = 4
RG_C = 8.0
C_GROUPS = 8
C_CHUNK = 128
X_HEADS = 4
N_GROUPS = 4
EXPERTS_PER_GROUP = 8
N_EXPERTS = N_GROUPS * EXPERTS_PER_GROUP

LANES = 128
VMEM_LIMIT = 56 * 1024 * 1024

F32 = jnp.float32
BF16 = jnp.bfloat16


def _cparams(sem, vmem=VMEM_LIMIT):
    return pltpu.CompilerParams(dimension_semantics=sem, vmem_limit_bytes=vmem)


def _const_spec(shape):
    nd = len(shape)
    return pl.BlockSpec(shape, lambda *_: (0,) * nd)


def _rms(x, g):
    ms = jnp.mean(x * x, axis=-1, keepdims=True)
    return x * lax.rsqrt(ms + EPS) * g


def _gelu(x):
    c = math.sqrt(2.0 / math.pi)
    return 0.5 * x * (1.0 + jnp.tanh(c * (x + 0.044715 * (x * x * x))))


def _sigmoid(x):
    return 1.0 / (1.0 + jnp.exp(-x))


def _softplus(x):
    return jnp.maximum(x, 0.0) + jnp.log1p(jnp.exp(-jnp.abs(x)))


def _dot(a, b):
    return jnp.dot(a, b, preferred_element_type=F32)


def _dot_nt(a, b):
    return lax.dot_general(a, b, (((1,), (1,)), ((), ())), preferred_element_type=F32)


def _split3(x):
    hi = x.astype(BF16).astype(F32)
    r = x - hi
    mid = r.astype(BF16).astype(F32)
    lo = (r - mid).astype(BF16).astype(F32)
    return hi, mid, lo


def _in_proj_kernel(x_ref, g_ref, w_ref, fb_ref, gain_ref, gmat_ref, peq_ref, pek_ref,
                    qp_ref, kp_ref, v_ref, xb_ref, gb_ref, carry_ref, *, tiles_per_seq):
    i = pl.program_id(0)
    tm = x_ref.shape[0]

    @pl.when(i % tiles_per_seq == 0)
    def _():
        carry_ref[...] = jnp.zeros_like(carry_ref)

    xn = _rms(x_ref[...], g_ref[...]).astype(BF16)
    proj = _dot(xn, w_ref[...])
    qk = proj[:, 0:1024]
    v_ref[...] = proj[:, 1024:1536].astype(BF16)
    fl = proj[:, 1536:1664]
    xb_ref[...] = proj[:, 1664:2176]
    gb_ref[...] = proj[:, 2176:2688]

    logf = -_softplus(-(fl + fb_ref[...]))
    row = lax.broadcasted_iota(jnp.int32, logf.shape, 0)
    c = logf
    s = 1
    while s < tm:
        c = c + jnp.where(row >= s, pltpu.roll(c, s, axis=0), 0.0)
        s *= 2
    c = c + carry_ref[...]
    carry_ref[...] = c[tm - 1:tm, :]

    hi, mid, lo = _split3(c)
    lane = lax.broadcasted_iota(jnp.int32, c.shape, 1)
    cpack = jnp.where(lane < 8, hi, jnp.where(lane < 16, mid, jnp.where(lane < 24, lo,
                      jnp.where(lane == 24, 1.0, 0.0)))).astype(BF16)
    ext_q = _dot(cpack, peq_ref[...])
    ext_k = _dot(cpack, pek_ref[...])

    gmat = gmat_ref[...]
    gain = gain_ref[...]
    lane_b = lax.broadcasted_iota(jnp.int32, (tm, LANES), 1)
    for h in range(A_HEADS):
        blk = qk[:, h * LANES:(h + 1) * LANES]
        sq = blk * blk
        sq_hi = sq.astype(BF16)
        sq_lo = (sq - sq_hi.astype(F32)).astype(BF16)
        ms = _dot(sq_hi, gmat) + _dot(sq_lo, gmat)
        nb = blk * lax.rsqrt(ms + EPS) * gain
        qp = jnp.where(lane_b < HEAD_DIM, nb, ext_q[:, h * LANES:(h + 1) * LANES])
        kp = jnp.where(lane_b < HEAD_DIM, pltpu.roll(nb, HEAD_DIM, axis=1),
                       ext_k[:, h * LANES:(h + 1) * LANES])
        qp_ref[:, h * LANES:(h + 1) * LANES] = qp.astype(BF16)
        kp_ref[:, h * LANES:(h + 1) * LANES] = kp.astype(BF16)


def _in_proj(x, g, w_cat, fb3, gain_qk, gmat, peq, pek, *, seq, tm):
    T, D = x.shape
    n = T // tm
    row = lambda w: pl.BlockSpec((tm, w), lambda i: (i, 0))
    return pl.pallas_call(
        functools.partial(_in_proj_kernel, tiles_per_seq=seq // tm),
        grid=(n,),
        in_specs=[row(D), _const_spec(g.shape), _const_spec(w_cat.shape), _const_spec(fb3.shape),
                  _const_spec(gain_qk.shape), _const_spec(gmat.shape), _const_spec(peq.shape),
                  _const_spec(pek.shape)],
        out_specs=[row(1024), row(1024), row(512), row(512), row(512)],
        out_shape=[jax.ShapeDtypeStruct((T, 1024), BF16), jax.ShapeDtypeStruct((T, 1024), BF16),
                   jax.ShapeDtypeStruct((T, 512), BF16), jax.ShapeDtypeStruct((T, 512), F32),
                   jax.ShapeDtypeStruct((T, 512), F32)],
        scratch_shapes=[pltpu.VMEM((1, LANES), F32)],
        compiler_params=_cparams(("arbitrary",)),
        name="in_proj_ab",
    )(x, g, w_cat, fb3, gain_qk, gmat, peq, pek)


def _fox_kernel(qi_ref, kj_ref, qp_ref, kp_ref, v_ref, o_ref, m_ref, l_ref, acc_ref):
    p_idx = pl.program_id(2)
    qi = qi_ref[p_idx]
    kj = kj_ref[p_idx]
    tq = qp_ref.shape[0]
    tk = kp_ref.shape[0]

    @pl.when(kj == 0)
    def _():
        m_ref[...] = jnp.full_like(m_ref, -jnp.inf)
        l_ref[...] = jnp.zeros_like(l_ref)
        acc_ref[...] = jnp.zeros_like(acc_ref)

    def step(masked):
        vpair = v_ref[...]
        for hh in range(2):
            q = qp_ref[:, hh * LANES:(hh + 1) * LANES]
            k = kp_ref[:, hh * LANES:(hh + 1) * LANES]
            s = _dot_nt(q, k)
            if masked:
                r = lax.broadcasted_iota(jnp.int32, (tq, tk), 0)
                c = lax.broadcasted_iota(jnp.int32, (tq, tk), 1)
                s = jnp.where(c <= r, s, -jnp.inf)
            m_prev = m_ref[hh]
            m_new = jnp.maximum(m_prev, jnp.max(s, axis=1, keepdims=True))
            alpha = jnp.exp(m_prev - m_new)
            p = jnp.exp(s - m_new)
            l_ref[hh] = alpha * l_ref[hh] + jnp.sum(p, axis=1, keepdims=True)
            acc_ref[hh] = alpha * acc_ref[hh] + _dot(p.astype(BF16), vpair)
            m_ref[hh] = m_new

    @pl.when(kj < qi)
    def _():
        step(False)

    @pl.when(kj == qi)
    def _():
        step(True)
        lane = lax.broadcasted_iota(jnp.int32, (tq, LANES), 1)
        o0 = acc_ref[0] / l_ref[0]
        o1 = acc_ref[1] / l_ref[1]
        o_ref[...] = jnp.where(lane < HEAD_DIM, o0, o1).astype(o_ref.dtype)


def _fox_attention(qp, kp, v, *, batch, seq, tq):
    T = batch * seq
    nq = seq // tq
    pairs = [(i, j) for i in range(nq) for j in range(i + 1)]
    qi_tab = jnp.array([p[0] for p in pairs], jnp.int32)
    kj_tab = jnp.array([p[1] for p in pairs], jnp.int32)
    grid_spec = pltpu.PrefetchScalarGridSpec(
        num_scalar_prefetch=2,
        grid=(batch, A_HEADS // 2, len(pairs)),
        in_specs=[
            pl.BlockSpec((tq, 2 * LANES), lambda b, hp, p, qi, kj: (b * nq + qi[p], hp)),
            pl.BlockSpec((tq, 2 * LANES), lambda b, hp, p, qi, kj: (b * nq + kj[p], hp)),
            pl.BlockSpec((tq, LANES), lambda b, hp, p, qi, kj: (b * nq + kj[p], hp)),
        ],
        out_specs=pl.BlockSpec((tq, LANES), lambda b, hp, p, qi, kj: (b * nq + qi[p], hp)),
        scratch_shapes=[pltpu.VMEM((2, tq, 1), F32), pltpu.VMEM((2, tq, 1), F32),
                        pltpu.VMEM((2, tq, LANES), F32)],
    )
    return pl.pallas_call(
        _fox_kernel,
        grid_spec=grid_spec,
        out_shape=jax.ShapeDtypeStruct((T, A_WIDTH), BF16),
        compiler_params=_cparams(("parallel", "parallel", "arbitrary")),
        name="fox_attention",
    )(qi_tab, kj_tab, qp, kp, v)


def _rglru_kernel(xb_ref, gb_ref, cw_ref, cb_ref, wg_ref, bg_ref, lam_ref, o_ref,
                  xs_ref, h_ref):
    i = pl.program_id(1)
    tt = xb_ref.shape[0]
    pad = 8

    @pl.when(i == 0)
    def _():
        xs_ref[0:pad, :] = jnp.zeros((pad, B_WIDTH), F32)
        h_ref[...] = jnp.zeros_like(h_ref)

    xs_ref[pad:pad + tt, :] = xb_ref[...]
    xc = cb_ref[...]
    for k in range(CONV_WIDTH):
        off = pad - (CONV_WIDTH - 1) + k
        xc = xc + cw_ref[k:k + 1, :] * xs_ref[off:off + tt, :]
    xs_ref[0:pad, :] = xs_ref[tt:tt + pad, :]

    pre = _dot(xc.astype(BF16), wg_ref[...]) + bg_ref[...]
    r = _sigmoid(pre[:, 0:B_WIDTH])
    ig = _sigmoid(pre[:, B_WIDTH:2 * B_WIDTH])
    log_a = (-RG_C) * r * _softplus(-lam_ref[...])
    a = jnp.exp(log_a)
    th = jnp.tanh(log_a)
    beta = jnp.sqrt(-2.0 * th / (1.0 - th))
    bv = beta * (ig * xc)

    row = lax.broadcasted_iota(jnp.int32, a.shape, 0)
    s = 1
    while s < tt:
        keep = row >= s
        a_sh = jnp.where(keep, pltpu.roll(a, s, axis=0), 1.0)
        b_sh = jnp.where(keep, pltpu.roll(bv, s, axis=0), 0.0)
        bv = a * b_sh + bv
        a = a * a_sh
        s *= 2
    h = bv + a * h_ref[...]
    h_ref[...] = h[tt - 1:tt, :]
    o_ref[...] = (_gelu(gb_ref[...]) * h).astype(o_ref.dtype)


def _rglru(xb, gb, conv_w, conv_b, w_gates, b_gates, lam, *, batch, seq, tt):
    T = batch * seq
    nt = seq // tt
    row = pl.BlockSpec((tt, B_WIDTH), lambda b, i: (b * nt + i, 0))
    return pl.pallas_call(
        _rglru_kernel,
        grid=(batch, nt),
        in_specs=[row, row, _const_spec(conv_w.shape), _const_spec(conv_b.shape),
                  _const_spec(w_gates.shape), _const_spec(b_gates.shape), _const_spec(lam.shape)],
        out_specs=row,
        out_shape=jax.ShapeDtypeStruct((T, B_WIDTH), BF16),
        scratch_shapes=[pltpu.VMEM((tt + 8, B_WIDTH), F32), pltpu.VMEM((1, B_WIDTH), F32)],
        compiler_params=_cparams(("parallel", "arbitrary")),
        name="rglru",
    )(xb, gb, conv_w, conv_b, w_gates, b_gates, lam)


def _out_proj_kernel(x_ref, a_ref, r_ref, w_ref, o_ref):
    o_ref[...] = (x_ref[...] + _dot(a_ref[...], w_ref[0:A_WIDTH, :])
                  + _dot(r_ref[...], w_ref[A_WIDTH:A_WIDTH + B_WIDTH, :]))


def _out_proj(x, attn, rec, w_out, *, tm):
    T, D = x.shape
    return pl.pallas_call(
        _out_proj_kernel,
        grid=(T // tm,),
        in_specs=[pl.BlockSpec((tm, D), lambda i: (i, 0)),
                  pl.BlockSpec((tm, A_WIDTH), lambda i: (i, 0)),
                  pl.BlockSpec((tm, B_WIDTH), lambda i: (i, 0)),
                  _const_spec(w_out.shape)],
        out_specs=pl.BlockSpec((tm, D), lambda i: (i, 0)),
        out_shape=jax.ShapeDtypeStruct((T, D), F32),
        compiler_params=_cparams(("parallel",)),
        name="out_proj_ab",
    )(x, attn, rec, w_out)


def _gmlp_kernel(x_ref, g_ref, wi_ref, bi_ref, vg_ref, ws_ref, bs_ref, wo_ref, o_ref):
    tm = x_ref.shape[0]
    cw = vg_ref.shape[1]
    gd = cw // C_GROUPS
    x = x_ref[...]
    xn = _rms(x, g_ref[...]).astype(BF16)
    u = _gelu(_dot(xn, wi_ref[:, 0:cw]) + bi_ref[:, 0:cw])
    v = _gelu(_dot(xn, wi_ref[:, cw:2 * cw]) + bi_ref[:, cw:2 * cw])
    vn = _rms(v, vg_ref[...]).astype(BF16)
    r = lax.broadcasted_iota(jnp.int32, (C_CHUNK, C_CHUNK), 0)
    c = lax.broadcasted_iota(jnp.int32, (C_CHUNK, C_CHUNK), 1)
    rows = []
    for ch in range(tm // C_CHUNK):
        cols = []
        for g in range(C_GROUPS):
            wt = jnp.where(c <= r, ws_ref[g], 0.0).astype(BF16)
            vv = vn[ch * C_CHUNK:(ch + 1) * C_CHUNK, g * gd:(g + 1) * gd]
            mixed = _dot(wt, vv) + bs_ref[:, g:g + 1]
            cols.append(u[ch * C_CHUNK:(ch + 1) * C_CHUNK, g * gd:(g + 1) * gd] * mixed)
        rows.append(jnp.concatenate(cols, axis=1))
    y = jnp.concatenate(rows, axis=0).astype(BF16)
    o_ref[...] = x + _dot(y, wo_ref[...])


def _gmlp(x, g, w_in, b_in, v_gain, w_s, b_s_t, w_out, *, tm):
    T, D = x.shape
    return pl.pallas_call(
        _gmlp_kernel,
        grid=(T // tm,),
        in_specs=[pl.BlockSpec((tm, D), lambda i: (i, 0)), _const_spec(g.shape),
                  _const_spec(w_in.shape), _const_spec(b_in.shape), _const_spec(v_gain.shape),
                  _const_spec(w_s.shape), _const_spec(b_s_t.shape), _const_spec(w_out.shape)],
        out_specs=pl.BlockSpec((tm, D), lambda i: (i, 0)),
        out_shape=jax.ShapeDtypeStruct((T, D), F32),
        compiler_params=_cparams(("parallel",)),
        name="gmlp_mixer",
    )(x, g, w_in, b_in, v_gain, w_s, b_s_t, w_out)


def _mem_kv_kernel(m_ref, g_ref, w_ref, kg_ref, k_ref, v_ref):
    D = m_ref.shape[1]
    hd = D // X_HEADS
    mn = _rms(m_ref[...], g_ref[...]).astype(BF16)
    kv = _dot(mn, w_ref[...])
    v_ref[...] = kv[:, D:2 * D].astype(BF16)
    for h in range(X_HEADS):
        k_ref[:, h * hd:(h + 1) * hd] = _rms(kv[:, h * hd:(h + 1) * hd], kg_ref[...]).astype(BF16)


def _mem_kv(mem, g, wkv, k_gain, *, mem_len):
    M, D = mem.shape
    row = pl.BlockSpec((mem_len, D), lambda b: (b, 0))
    return pl.pallas_call(
        _mem_kv_kernel,
        grid=(M // mem_len,),
        in_specs=[row, _const_spec(g.shape), _const_spec(wkv.shape), _const_spec(k_gain.shape)],
        out_specs=[row, row],
        out_shape=[jax.ShapeDtypeStruct((M, D), BF16), jax.ShapeDtypeStruct((M, D), BF16)],
        compiler_params=_cparams(("parallel",)),
        name="mem_kv",
    )(mem, g, wkv, k_gain)


def _cross_router_kernel(x_ref, gx_ref, wq_ref, qg_ref, k_ref, v_ref, wo_ref, gf_ref,
                         wrh_ref, wrm_ref, br_ref,
                         x2_ref, xn_ref, slab_ref, cnt_ref, carry_ref):
    i = pl.program_id(0)
    tm, D = x_ref.shape
    hd = D // X_HEADS

    @pl.when(i == 0)
    def _():
        carry_ref[...] = jnp.zeros_like(carry_ref)

    x = x_ref[...]
    xn = _rms(x, gx_ref[...]).astype(BF16)
    q = _dot(xn, wq_ref[...])
    scale = hd ** -0.5
    outs = []
    for h in range(X_HEADS):
        qh = _rms(q[:, h * hd:(h + 1) * hd], qg_ref[...]) * scale
        s = _dot_nt(qh.astype(BF16), k_ref[:, h * hd:(h + 1) * hd])
        m = jnp.max(s, axis=1, keepdims=True)
        p = jnp.exp(s - m)
        p = p / jnp.sum(p, axis=1, keepdims=True)
        outs.append(_dot(p.astype(BF16), v_ref[:, h * hd:(h + 1) * hd]))
    o = jnp.concatenate(outs, axis=1).astype(BF16)
    x2 = x + _dot(o, wo_ref[...])
    x2_ref[...] = x2

    xf = _rms(x2, gf_ref[...])
    xn_ref[...] = xf
    xh = xf.astype(BF16)
    xm = (xf - xh.astype(F32)).astype(BF16)
    wh = wrh_ref[...]
    logits = _dot(xh, wh) + (_dot(xh, wrm_ref[...]) + _dot(xm, wh)) + br_ref[...]

    lane = lax.broadcasted_iota(jnp.int32, logits.shape, 1).astype(F32)
    big = float(LANES)
    ninf = -jnp.inf
    glog = jnp.where(lane < N_GROUPS, logits, ninf)
    gmax = jnp.max(glog, axis=1, keepdims=True)
    gexp = jnp.exp(glog - gmax)
    gprob = gexp / jnp.sum(gexp, axis=1, keepdims=True)
    g_w = jnp.max(gprob, axis=1, keepdims=True)
    g_idx = jnp.min(jnp.where(gprob == g_w, lane, big), axis=1, keepdims=True)
    lo = N_GROUPS + EXPERTS_PER_GROUP * g_idx
    sel = jnp.logical_and(lane >= lo, lane < lo + EXPERTS_PER_GROUP)
    es = jnp.where(sel, logits, ninf)
    v1 = jnp.max(es, axis=1, keepdims=True)
    i1 = jnp.min(jnp.where(es == v1, lane, big), axis=1, keepdims=True)
    es2 = jnp.where(lane == i1, ninf, es)
    v2 = jnp.max(es2, axis=1, keepdims=True)
    i2 = jnp.min(jnp.where(es2 == v2, lane, big), axis=1, keepdims=True)
    t = jnp.exp(v2 - v1)
    w0 = g_w * (1.0 / (1.0 + t))
    w1 = g_w * (t / (1.0 + t))
    e0 = i1 - N_GROUPS
    e1 = i2 - N_GROUPS

    oh0 = lane == e0
    oh1 = lane == e1
    oh = jnp.where(jnp.logical_or(oh0, oh1), 1.0, 0.0)
    rr = lax.broadcasted_iota(jnp.int32, (tm, tm), 0)
    cc = lax.broadcasted_iota(jnp.int32, (tm, tm), 1)
    tri = jnp.where(cc < rr, 1.0, 0.0).astype(BF16)
    carry = carry_ref[...]
    before = _dot(tri, oh.astype(BF16)) + carry
    r0 = jnp.sum(jnp.where(oh0, before, 0.0), axis=1, keepdims=True)
    r1 = jnp.sum(jnp.where(oh1, before, 0.0), axis=1, keepdims=True)
    carry = carry + jnp.sum(oh, axis=0, keepdims=True)
    carry_ref[...] = carry
    cnt_ref[...] = jnp.broadcast_to(carry, cnt_ref.shape)

    slab = jnp.where(lane == 0, e0,
           jnp.where(lane == 1, e1,
           jnp.where(lane == 2, r0,
           jnp.where(lane == 3, r1,
           jnp.where(lane == 4, w0,
           jnp.where(lane == 5, w1, 0.0))))))
    slab_ref[...] = slab


def _cross_router(x, gx, wq, q_gain, kmem, vmem, wo, gf, wr_hi, wr_mid, br, *, seq, mem_len, tm):
    T, D = x.shape
    tiles_per_seq = seq // tm
    row = lambda w: pl.BlockSpec((tm, w), lambda i: (i, 0))
    memspec = pl.BlockSpec((mem_len, D), lambda i: (i // tiles_per_seq, 0))
    return pl.pallas_call(
        _cross_router_kernel,
        grid=(T // tm,),
        in_specs=[row(D), _const_spec(gx.shape), _const_spec(wq.shape), _const_spec(q_gain.shape),
                  memspec, memspec, _const_spec(wo.shape), _const_spec(gf.shape),
                  _const_spec(wr_hi.shape), _const_spec(wr_mid.shape), _const_spec(br.shape)],
        out_specs=[row(D), row(D), row(LANES), _const_spec((8, LANES))],
        out_shape=[jax.ShapeDtypeStruct((T, D), F32), jax.ShapeDtypeStruct((T, D), F32),
                   jax.ShapeDtypeStruct((T, LANES), F32), jax.ShapeDtypeStruct((8, LANES), F32)],
        scratch_shapes=[pltpu.VMEM((1, LANES), F32)],
        compiler_params=_cparams(("arbitrary",)),
        name="cross_attn_router",
    )(x, gx, wq, q_gain, kmem, vmem, wo, gf, wr_hi, wr_mid, br)


def _expert_kernel(te_ref, nv_ref, tok_ref, tokn_ref, x_hbm, wg_ref, wu_ref, wd_ref, y_ref,
                   xbuf, sem):
    t = pl.program_id(0)
    tm = y_ref.shape[0]
    n_valid = nv_ref[0]
    slot = t % 2

    def row_copy(src_row, dst_slot, r):
        return pltpu.make_async_copy(x_hbm.at[pl.ds(src_row, 1)],
                                     xbuf.at[dst_slot, pl.ds(r, 1)], sem.at[dst_slot])

    def issue(idx_ref, dst_slot):
        def body(r, carry):
            row_copy(idx_ref[0, 0, r], dst_slot, r).start()
            return carry
        lax.fori_loop(0, tm, body, 0, unroll=8)

    @pl.when(jnp.logical_and(t == 0, n_valid > 0))
    def _():
        issue(tok_ref, 0)

    @pl.when(t + 1 < n_valid)
    def _():
        issue(tokn_ref, 1 - slot)

    @pl.when(t < n_valid)
    def _():
        def wbody(r, carry):
            row_copy(0, slot, r).wait()
            return carry
        lax.fori_loop(0, tm, wbody, 0, unroll=8)
        x = xbuf[slot].astype(BF16)
        g = _dot(x, wg_ref[0].astype(BF16))
        u = _dot(x, wu_ref[0].astype(BF16))
        h = (g * _sigmoid(g) * u).astype(BF16)
        y_ref[...] = _dot(h, wd_ref[0].astype(BF16))

    @pl.when(t >= n_valid)
    def _():
        y_ref[...] = jnp.zeros_like(y_ref)


def _experts(tile_expert, n_valid, tok_sorted, xn, w_gate, w_up, w_down, *, tm):
    n_tiles = tile_expert.shape[0]
    T, D = xn.shape
    de = w_gate.shape[2]
    tok3 = tok_sorted.reshape(n_tiles, 1, tm)
    grid_spec = pltpu.PrefetchScalarGridSpec(
        num_scalar_prefetch=2,
        grid=(n_tiles,),
        in_specs=[
            pl.BlockSpec((1, 1, tm), lambda t, te, nv: (t, 0, 0), memory_space=pltpu.SMEM),
            pl.BlockSpec((1, 1, tm), lambda t, te, nv: (jnp.minimum(t + 1, n_tiles - 1), 0, 0),
                         memory_space=pltpu.SMEM),
            pl.BlockSpec(memory_space=pl.ANY),
            pl.BlockSpec((1, D, de), lambda t, te, nv: (te[t], 0, 0)),
            pl.BlockSpec((1, D, de), lambda t, te, nv: (te[t], 0, 0)),
            pl.BlockSpec((1, de, D), lambda t, te, nv: (te[t], 0, 0)),
        ],
        out_specs=pl.BlockSpec((tm, D), lambda t, te, nv: (t, 0)),
        scratch_shapes=[pltpu.VMEM((2, tm, D), F32), pltpu.SemaphoreType.DMA((2,))],
    )
    return pl.pallas_call(
        _expert_kernel,
        grid_spec=grid_spec,
        out_shape=jax.ShapeDtypeStruct((n_tiles * tm, D), F32),
        compiler_params=_cparams(("arbitrary",)),
        name="moe_experts",
    )(tile_expert, n_valid, tok3, tok3, xn, w_gate, w_up, w_down)


def _combine_kernel(pos_ref, posn_ref, x_ref, slab_ref, y_hbm, o_ref, ybuf, sem):
    i = pl.program_id(0)
    n = pl.num_programs(0)
    tc = x_ref.shape[0]
    slot = i % 2

    def row_copy(src_row, dst_slot, r):
        return pltpu.make_async_copy(y_hbm.at[pl.ds(src_row, 1)],
                                     ybuf.at[dst_slot, pl.ds(r, 1)], sem.at[dst_slot])

    def issue(idx_ref, dst_slot):
        def body(r, carry):
            row_copy(idx_ref[0, 0, r], dst_slot, r).start()
            return carry
        lax.fori_loop(0, 2 * tc, body, 0, unroll=8)

    @pl.when(i == 0)
    def _():
        issue(pos_ref, 0)

    @pl.when(i + 1 < n)
    def _():
        issue(posn_ref, 1 - slot)

    def wbody(r, carry):
        row_copy(0, slot, r).wait()
        return carry
    lax.fori_loop(0, 2 * tc, wbody, 0, unroll=8)

    slab = slab_ref[...]
    w0 = slab[:, 4:5]
    w1 = slab[:, 5:6]
    o_ref[...] = x_ref[...] + (w0 * ybuf[slot, 0:tc, :] + w1 * ybuf[slot, tc:2 * tc, :])


def _combine(pos_km, x, slab, y, *, tc):
    T, D = x.shape
    n = T // tc
    pos3 = pos_km.reshape(n, 1, 2 * tc)
    return pl.pallas_call(
        _combine_kernel,
        grid=(n,),
        in_specs=[
            pl.BlockSpec((1, 1, 2 * tc), lambda i: (i, 0, 0), memory_space=pltpu.SMEM),
            pl.BlockSpec((1, 1, 2 * tc), lambda i: (jnp.minimum(i + 1, n - 1), 0, 0),
                         memory_space=pltpu.SMEM),
            pl.BlockSpec((tc, D), lambda i: (i, 0)),
            pl.BlockSpec((tc, LANES), lambda i: (i, 0)),
            pl.BlockSpec(memory_space=pl.ANY),
        ],
        out_specs=pl.BlockSpec((tc, D), lambda i: (i, 0)),
        out_shape=jax.ShapeDtypeStruct((T, D), F32),
        scratch_shapes=[pltpu.VMEM((2, 2 * tc, D), F32), pltpu.SemaphoreType.DMA((2,))],
        compiler_params=_cparams(("arbitrary",)),
        name="moe_combine",
    )(pos3, pos3, x, slab, y)


def _row(v):
    return v.reshape(1, -1).astype(F32)


def _pack_ab_weights(w_in, f_bias, q_gain, k_gain):
    D = w_in.shape[0]
    wq = w_in[:, 0:A_WIDTH].reshape(D, A_HEADS, HEAD_DIM)
    wk = w_in[:, A_WIDTH:2 * A_WIDTH].reshape(D, A_HEADS, HEAD_DIM)
    wqk = jnp.stack([wq, wk], axis=2).reshape(D, 2 * A_WIDTH)
    wv = w_in[:, 2 * A_WIDTH:3 * A_WIDTH]
    o = 3 * A_WIDTH
    wf = w_in[:, o:o + A_HEADS]
    wf3 = jnp.concatenate([wf, wf, wf, jnp.zeros((D, LANES - 3 * A_HEADS), F32)], axis=1)
    wxb = w_in[:, o + A_HEADS:o + A_HEADS + B_WIDTH]
    wgb = w_in[:, o + A_HEADS + B_WIDTH:o + A_HEADS + 2 * B_WIDTH]
    w_cat = jnp.concatenate([wqk, wv, wf3, wxb, wgb], axis=1).astype(BF16)
    fb3 = jnp.concatenate([f_bias, f_bias, f_bias, jnp.zeros((LANES - 3 * A_HEADS,), F32)]).reshape(1, LANES)
    gain_qk = jnp.concatenate([q_gain * (HEAD_DIM ** -0.5), k_gain]).reshape(1, LANES)
    return w_cat, fb3, gain_qk


def _decay_placement():
    peq = [[0.0] * (A_HEADS * LANES) for _ in range(LANES)]
    pek = [[0.0] * (A_HEADS * LANES) for _ in range(LANES)]
    for h in range(A_HEADS):
        base = h * LANES + HEAD_DIM
        for part in range(3):
            peq[part * 8 + h][base + part] = 1.0
            peq[24][base + 3 + part] = 1.0
            pek[24][base + part] = 1.0
            pek[part * 8 + h][base + 3 + part] = -1.0
    gm = [[(1.0 / HEAD_DIM) if (r // HEAD_DIM) == (c // HEAD_DIM) else 0.0 for c in range(LANES)]
          for r in range(LANES)]
    return (jnp.array(peq, BF16), jnp.array(pek, BF16), jnp.array(gm, BF16))


def _block_diag(w):
    nb, bi, bo = w.shape
    eye = jnp.eye(nb, dtype=w.dtype)
    return (eye[:, None, :, None] * w[:, :, None, :]).reshape(nb * bi, nb * bo)


def _routing(slab, counts, *, tm, n_tiles):
    T = slab.shape[0]
    e = slab[:, 0:2].astype(jnp.int32)
    r = slab[:, 2:4].astype(jnp.int32)
    cnt = counts[0, :N_EXPERTS].astype(jnp.int32)
    padded = ((cnt + tm - 1) // tm) * tm
    ends = jnp.cumsum(padded)
    offs = ends - padded
    pos = offs[e] + r
    slot_tok = jnp.broadcast_to(jnp.arange(T, dtype=jnp.int32)[:, None], (T, 2))
    tok_sorted = jnp.zeros((n_tiles * tm,), jnp.int32).at[pos.reshape(-1)].set(slot_tok.reshape(-1))
    n_valid = (ends[-1] // tm).astype(jnp.int32)
    starts = jnp.arange(n_tiles, dtype=jnp.int32) * tm
    te = jnp.searchsorted(ends, starts, side="right").astype(jnp.int32)
    last = jnp.take(te, jnp.maximum(n_valid - 1, 0))
    te = jnp.where(jnp.arange(n_tiles) < n_valid, jnp.minimum(te, N_EXPERTS - 1), last)
    return pos, tok_sorted, te, n_valid.reshape(1)


def _moe(x2, xn, slab, counts, w_gate, w_up, w_down, *, tm_e, tc):
    T = x2.shape[0]
    n_tiles = (2 * T) // tm_e + N_EXPERTS
    pos, tok_sorted, te, n_valid = _routing(slab, counts, tm=tm_e, n_tiles=n_tiles)
    y = _experts(te, n_valid, tok_sorted, xn, w_gate, w_up, w_down, tm=tm_e)
    pos_km = pos.reshape(T // tc, tc, 2).transpose(0, 2, 1).reshape(-1)
    return _combine(pos_km, x2, slab, y, tc=tc)


def _cross_and_moe(x, mem2, layer, norm_cross, norm_mem, norm_ffn, x_wq, x_wkv, x_wo, x_q_gain,
                   x_k_gain, moe_wg, moe_bg, moe_we, moe_be, moe_w_gate, moe_w_up, moe_w_down,
                   *, seq, mem_len, tm, tm_e, tc):
    kmem, vmem = _mem_kv(mem2, _row(norm_mem[layer]), x_wkv[layer].astype(BF16),
                         _row(x_k_gain[layer]), mem_len=mem_len)
    D = x.shape[1]
    wr = jnp.concatenate([moe_wg[layer], moe_we[layer],
                          jnp.zeros((D, LANES - N_GROUPS - N_EXPERTS), F32)], axis=1)
    wr_hi = wr.astype(BF16)
    wr_mid = (wr - wr_hi.astype(F32)).astype(BF16)
    br = jnp.concatenate([moe_bg[layer], moe_be[layer],
                          jnp.zeros((LANES - N_GROUPS - N_EXPERTS,), F32)]).reshape(1, LANES)
    x2, xn, slab, counts = _cross_router(
        x, _row(norm_cross[layer]), x_wq[layer].astype(BF16), _row(x_q_gain[layer]), kmem, vmem,
        x_wo[layer].astype(BF16), _row(norm_ffn[layer]), wr_hi, wr_mid, br,
        seq=seq, mem_len=mem_len, tm=tm)
    return _moe(x2, xn, slab, counts, moe_w_gate[layer], moe_w_up[layer], moe_w_down[layer],
                tm_e=tm_e, tc=tc)


def kernel(x, mem, norm_mix, norm_cross, norm_mem, norm_ffn, ab_w_in, ab_f_bias, ab_q_gain, ab_k_gain, ab_conv_w, ab_conv_b, ab_wa, ab_ba, ab_wx, ab_bx, ab_lambda, ab_w_out, c_w_in, c_b_in, c_v_gain, c_w_s, c_b_s, c_w_out, x_wq, x_wkv, x_wo, x_q_gain, x_k_gain, moe_wg, moe_bg, moe_we, moe_be, moe_w_gate, moe_w_up, moe_w_down):
    B, S, D = x.shape
    mem_len = mem.shape[1]
    T = B * S
    tm = min(512, S)
    x2d = x.reshape(T, D)
    mem2 = mem.reshape(B * mem_len, D)
    common = dict(seq=S, mem_len=mem_len, tm=tm, tm_e=min(512, T), tc=min(256, T))
    tail = (norm_cross, norm_mem, norm_ffn, x_wq, x_wkv, x_wo, x_q_gain, x_k_gain,
            moe_wg, moe_bg, moe_we, moe_be, moe_w_gate, moe_w_up, moe_w_down)

    w_cat, fb3, gain_qk = _pack_ab_weights(ab_w_in[0], ab_f_bias[0], ab_q_gain[0], ab_k_gain[0])
    peq, pek, gmat = _decay_placement()
    qp, kp, v, xb, gb = _in_proj(x2d, _row(norm_mix[0]), w_cat, fb3, gain_qk, gmat, peq, pek,
                                 seq=S, tm=tm)
    attn = _fox_attention(qp, kp, v, batch=B, seq=S, tq=tm)
    w_gates = jnp.concatenate([_block_diag(ab_wa[0]), _block_diag(ab_wx[0])], axis=1).astype(BF16)
    b_gates = jnp.concatenate([ab_ba[0], ab_bx[0]]).reshape(1, -1)
    rec = _rglru(xb, gb, ab_conv_w[0], _row(ab_conv_b[0]), w_gates, b_gates, _row(ab_lambda[0]),
                 batch=B, seq=S, tt=min(256, S))
    x2d = _out_proj(x2d, attn, rec, ab_w_out[0].astype(BF16), tm=tm)
    x2d = _cross_and_moe(x2d, mem2, 0, *tail, **common)

    x2d = _gmlp(x2d, _row(norm_mix[1]), c_w_in[0].astype(BF16), _row(c_b_in[0]), _row(c_v_gain[0]),
                c_w_s[0], c_b_s[0].T, c_w_out[0].astype(BF16), tm=min(256, S))
    x2d = _cross_and_moe(x2d, mem2, 1, *tail, **common)
    return x2d.reshape(B, S, D)
```

```python
import functools
import math

import jax
import jax.numpy as jnp
from jax import lax
from jax.experimental import pallas as pl
from jax.experimental.pallas import tpu as pltpu

EPS = 1e-6
A_HEADS = 8
HEAD_DIM = 64
A_WIDTH = A_HEADS * HEAD_DIM
B_WIDTH = 512
B_BLOCKS = 8
CONV_WIDTH = 4
RG_C = 8.0
C_GROUPS = 8
C_CHUNK = 128
X_HEADS = 4
N_GROUPS = 4
EXPERTS_PER_GROUP = 8
N_EXPERTS = N_GROUPS * EXPERTS_PER_GROUP

LOG2E = 1.4426950408889634
LANES = 128
VMEM_LIMIT = 56 * 1024 * 1024

F32 = jnp.float32
BF16 = jnp.bfloat16


def _cparams(sem, vmem=VMEM_LIMIT):
    return pltpu.CompilerParams(dimension_semantics=sem, vmem_limit_bytes=vmem)


def _const_spec(shape):
    nd = len(shape)
    return pl.BlockSpec(shape, lambda *_: (0,) * nd)


def _rms(x, g):
    ms = jnp.mean(x * x, axis=-1, keepdims=True)
    return x * lax.rsqrt(ms + EPS) * g


def _gelu(x):
    c = math.sqrt(2.0 / math.pi)
    return 0.5 * x * (1.0 + jnp.tanh(c * (x + 0.044715 * (x * x * x))))


def _sigmoid(x):
    return 1.0 / (1.0 + jnp.exp(-x))


def _softplus(x):
    return jnp.maximum(x, 0.0) + jnp.log1p(jnp.exp(-jnp.abs(x)))


def _dot(a, b):
    return jnp.dot(a, b, preferred_element_type=F32)


def _dot_nt(a, b):
    return lax.dot_general(a, b, (((1,), (1,)), ((), ())), preferred_element_type=F32)


def _split3(x):
    hi = x.astype(BF16).astype(F32)
    r = x - hi
    mid = r.astype(BF16).astype(F32)
    lo = (r - mid).astype(BF16).astype(F32)
    return hi, mid, lo


ROW_TILE = 8


def _store_tok_tiles(ref, val):
    m = val.shape[0]
    for j in range(ROW_TILE):
        ref[pl.ds(j, m, stride=ROW_TILE), :] = val[:, j * LANES:(j + 1) * LANES]


def _load_tok_cols(ref, slot, lo, n, j):
    return ref[slot, pl.ds(lo * ROW_TILE + j, n, stride=ROW_TILE), :]


def _in_proj_kernel(x_ref, g_ref, w_ref, fb_ref, gain_ref, gmat_ref, peq_ref, pek_ref,
                    qp_ref, kp_ref, v_ref, xb_ref, gb_ref, carry_ref, *, tiles_per_seq):
    i = pl.program_id(0)
    tm = x_ref.shape[0]

    @pl.when(i % tiles_per_seq == 0)
    def _():
        carry_ref[...] = jnp.zeros_like(carry_ref)

    xn = _rms(x_ref[...], g_ref[...]).astype(BF16)
    proj = _dot(xn, w_ref[...])
    qk = proj[:, 0:1024]
    v_ref[...] = proj[:, 1024:1536].astype(BF16)
    fl = proj[:, 1536:1664]
    xb_ref[...] = proj[:, 1664:2176]
    gb_ref[...] = proj[:, 2176:2688]

    logf = -_softplus(-(fl + fb_ref[...]))
    row = lax.broadcasted_iota(jnp.int32, logf.shape, 0)
    c = logf
    s = 1
    while s < tm:
        c = c + jnp.where(row >= s, pltpu.roll(c, s, axis=0), 0.0)
        s *= 2
    c = c + carry_ref[...]
    carry_ref[...] = c[tm - 1:tm, :]

    hi, mid, lo = _split3(c * LOG2E)
    lane = lax.broadcasted_iota(jnp.int32, c.shape, 1)
    cpack = jnp.where(lane < 8, hi, jnp.where(lane < 16, mid, jnp.where(lane < 24, lo,
                      jnp.where(lane == 24, 1.0, 0.0)))).astype(BF16)
    ext_q = _dot(cpack, peq_ref[...])
    ext_k = _dot(cpack, pek_ref[...])

    gmat = gmat_ref[...]
    gain = gain_ref[...]
    lane_b = lax.broadcasted_iota(jnp.int32, (tm, LANES), 1)
    for h in range(A_HEADS):
        blk = qk[:, h * LANES:(h + 1) * LANES]
        sq = blk * blk
        sq_hi = sq.astype(BF16)
        sq_lo = (sq - sq_hi.astype(F32)).astype(BF16)
        ms = _dot(sq_hi, gmat) + _dot(sq_lo, gmat)
        nb = blk * lax.rsqrt(ms + EPS) * gain
        qp = jnp.where(lane_b < HEAD_DIM, nb, ext_q[:, h * LANES:(h + 1) * LANES])
        kp = jnp.where(lane_b < HEAD_DIM, pltpu.roll(nb, HEAD_DIM, axis=1),
                       ext_k[:, h * LANES:(h + 1) * LANES])
        qp_ref[:, h * LANES:(h + 1) * LANES] = qp.astype(BF16)
        kp_ref[:, h * LANES:(h + 1) * LANES] = kp.astype(BF16)


def _in_proj(x, g, w_cat, fb3, gain_qk, gmat, peq, pek, *, seq, tm):
    T, D = x.shape
    n = T // tm
    row = lambda w: pl.BlockSpec((tm, w), lambda i: (i, 0))
    return pl.pallas_call(
        functools.partial(_in_proj_kernel, tiles_per_seq=seq // tm),
        grid=(n,),
        in_specs=[row(D), _const_spec(g.shape), _const_spec(w_cat.shape), _const_spec(fb3.shape),
                  _const_spec(gain_qk.shape), _const_spec(gmat.shape), _const_spec(peq.shape),
                  _const_spec(pek.shape)],
        out_specs=[row(1024), row(1024), row(512), row(512), row(512)],
        out_shape=[jax.ShapeDtypeStruct((T, 1024), BF16), jax.ShapeDtypeStruct((T, 1024), BF16),
                   jax.ShapeDtypeStruct((T, 512), BF16), jax.ShapeDtypeStruct((T, 512), F32),
                   jax.ShapeDtypeStruct((T, 512), F32)],
        scratch_shapes=[pltpu.VMEM((1, LANES), F32)],
        compiler_params=_cparams(("arbitrary",)),
        name="in_proj_ab",
    )(x, g, w_cat, fb3, gain_qk, gmat, peq, pek)


def _fox_kernel(q_ref, k_ref, v_ref, o_ref, m0, l0, a0, m1, l1, a1):
    i = pl.program_id(2)
    tq = q_ref.shape[0]
    stats = ((m0, l0, a0), (m1, l1, a1))
    for m_ref, l_ref, a_ref in stats:
        m_ref[...] = jnp.full_like(m_ref, -jnp.inf)
        l_ref[...] = jnp.zeros_like(l_ref)
        a_ref[...] = jnp.zeros_like(a_ref)

    def tile(j, masked):
        rows = pl.ds(pl.multiple_of(j * tq, tq), tq)
        vpair = v_ref[rows, :]
        for hh in range(2):
            m_ref, l_ref, a_ref = stats[hh]
            q = q_ref[:, hh * LANES:(hh + 1) * LANES]
            k = k_ref[rows, hh * LANES:(hh + 1) * LANES]
            s = _dot_nt(q, k)
            if masked:
                r = lax.broadcasted_iota(jnp.int32, (tq, tq), 0)
                c = lax.broadcasted_iota(jnp.int32, (tq, tq), 1)
                s = jnp.where(c <= r, s, -jnp.inf)
            m_prev = m_ref[...]
            m_new = jnp.maximum(m_prev, jnp.max(s, axis=1, keepdims=True))
            alpha = jnp.exp2(m_prev - m_new)
            p = jnp.exp2(s - jnp.concatenate([m_new] * (tq // LANES), axis=1))
            l_ref[...] = alpha * l_ref[...] + jnp.sum(p, axis=1, keepdims=True)
            a_ref[...] = alpha * a_ref[...] + _dot(p.astype(BF16), vpair)
            m_ref[...] = m_new

    def body(j, carry):
        tile(j, False)
        return carry
    lax.fori_loop(0, i, body, 0)
    tile(i, True)
    lane = lax.broadcasted_iota(jnp.int32, (tq, LANES), 1)
    o_ref[...] = jnp.where(lane < HEAD_DIM, a0[...] / l0[...], a1[...] / l1[...]).astype(o_ref.dtype)


def _fox_attention(qp, kp, v, *, batch, seq, tq):
    T = batch * seq
    nq = seq // tq
    stat = pltpu.VMEM((tq, LANES), F32)
    return pl.pallas_call(
        _fox_kernel,
        grid=(batch, A_HEADS // 2, nq),
        in_specs=[
            pl.BlockSpec((tq, 2 * LANES), lambda b, hp, i: (b * nq + i, hp)),
            pl.BlockSpec((seq, 2 * LANES), lambda b, hp, i: (b, hp)),
            pl.BlockSpec((seq, LANES), lambda b, hp, i: (b, hp)),
        ],
        out_specs=pl.BlockSpec((tq, LANES), lambda b, hp, i: (b * nq + i, hp)),
        out_shape=jax.ShapeDtypeStruct((T, A_WIDTH), BF16),
        scratch_shapes=[stat] * 6,
        compiler_params=_cparams(("parallel", "parallel", "arbitrary")),
        name="fox_attention",
    )(qp, kp, v)


def _rglru_kernel(xb_ref, gb_ref, cw_ref, cb_ref, wg_ref, bg_ref, lam_ref, o_ref,
                  xs_ref, h_ref):
    i = pl.program_id(1)
    tt = xb_ref.shape[0]
    pad = 8

    @pl.when(i == 0)
    def _():
        xs_ref[0:pad, :] = jnp.zeros((pad, B_WIDTH), F32)
        h_ref[...] = jnp.zeros_like(h_ref)

    xs_ref[pad:pad + tt, :] = xb_ref[...]
    xc = cb_ref[...]
    for k in range(CONV_WIDTH):
        off = pad - (CONV_WIDTH - 1) + k
        xc = xc + cw_ref[k:k + 1, :] * xs_ref[off:off + tt, :]
    xs_ref[0:pad, :] = xs_ref[tt:tt + pad, :]

    pre = _dot(xc.astype(BF16), wg_ref[...]) + bg_ref[...]
    r = _sigmoid(pre[:, 0:B_WIDTH])
    ig = _sigmoid(pre[:, B_WIDTH:2 * B_WIDTH])
    log_a = (-RG_C) * r * _softplus(-lam_ref[...])
    a = jnp.exp(log_a)
    th = jnp.tanh(log_a)
    beta = jnp.sqrt(-2.0 * th / (1.0 - th))
    bv = beta * (ig * xc)

    row = lax.broadcasted_iota(jnp.int32, a.shape, 0)
    s = 1
    while s < tt:
        keep = row >= s
        a_sh = jnp.where(keep, pltpu.roll(a, s, axis=0), 1.0)
        b_sh = jnp.where(keep, pltpu.roll(bv, s, axis=0), 0.0)
        bv = a * b_sh + bv
        a = a * a_sh
        s *= 2
    h = bv + a * h_ref[...]
    h_ref[...] = h[tt - 1:tt, :]
    o_ref[...] = (_gelu(gb_ref[...]) * h).astype(o_ref.dtype)


def _rglru(xb, gb, conv_w, conv_b, w_gates, b_gates, lam, *, batch, seq, tt):
    T = batch * seq
    nt = seq // tt
    row = pl.BlockSpec((tt, B_WIDTH), lambda b, i: (b * nt + i, 0))
    return pl.pallas_call(
        _rglru_kernel,
        grid=(batch, nt),
        in_specs=[row, row, _const_spec(conv_w.shape), _const_spec(conv_b.shape),
                  _const_spec(w_gates.shape), _const_spec(b_gates.shape), _const_spec(lam.shape)],
        out_specs=row,
        out_shape=jax.ShapeDtypeStruct((T, B_WIDTH), BF16),
        scratch_shapes=[pltpu.VMEM((tt + 8, B_WIDTH), F32), pltpu.VMEM((1, B_WIDTH), F32)],
        compiler_params=_cparams(("parallel", "arbitrary")),
        name="rglru",
    )(xb, gb, conv_w, conv_b, w_gates, b_gates, lam)


def _out_proj_kernel(x_ref, a_ref, r_ref, w_ref, o_ref):
    o_ref[...] = (x_ref[...] + _dot(a_ref[...], w_ref[0:A_WIDTH, :])
                  + _dot(r_ref[...], w_ref[A_WIDTH:A_WIDTH + B_WIDTH, :]))


def _out_proj(x, attn, rec, w_out, *, tm):
    T, D = x.shape
    return pl.pallas_call(
        _out_proj_kernel,
        grid=(T // tm,),
        in_specs=[pl.BlockSpec((tm, D), lambda i: (i, 0)),
                  pl.BlockSpec((tm, A_WIDTH), lambda i: (i, 0)),
                  pl.BlockSpec((tm, B_WIDTH), lambda i: (i, 0)),
                  _const_spec(w_out.shape)],
        out_specs=pl.BlockSpec((tm, D), lambda i: (i, 0)),
        out_shape=jax.ShapeDtypeStruct((T, D), F32),
        compiler_params=_cparams(("parallel",)),
        name="out_proj_ab",
    )(x, attn, rec, w_out)


def _gmlp_kernel(x_ref, g_ref, wi_ref, bi_ref, vg_ref, ws_ref, bs_ref, wo_ref, o_ref):
    tm = x_ref.shape[0]
    cw = vg_ref.shape[1]
    gd = cw // C_GROUPS
    x = x_ref[...]
    xn = _rms(x, g_ref[...]).astype(BF16)
    u = _gelu(_dot(xn, wi_ref[:, 0:cw]) + bi_ref[:, 0:cw])
    v = _gelu(_dot(xn, wi_ref[:, cw:2 * cw]) + bi_ref[:, cw:2 * cw])
    vn = _rms(v, vg_ref[...]).astype(BF16)
    r = lax.broadcasted_iota(jnp.int32, (C_CHUNK, C_CHUNK), 0)
    c = lax.broadcasted_iota(jnp.int32, (C_CHUNK, C_CHUNK), 1)
    rows = []
    for ch in range(tm // C_CHUNK):
        cols = []
        for g in range(C_GROUPS):
            wt = jnp.where(c <= r, ws_ref[g], 0.0).astype(BF16)
            vv = vn[ch * C_CHUNK:(ch + 1) * C_CHUNK, g * gd:(g + 1) * gd]
            mixed = _dot(wt, vv) + bs_ref[:, g:g + 1]
            cols.append(u[ch * C_CHUNK:(ch + 1) * C_CHUNK, g * gd:(g + 1) * gd] * mixed)
        rows.append(jnp.concatenate(cols, axis=1))
    y = jnp.concatenate(rows, axis=0).astype(BF16)
    o_ref[...] = x + _dot(y, wo_ref[...])


def _gmlp(x, g, w_in, b_in, v_gain, w_s, b_s_t, w_out, *, tm):
    T, D = x.shape
    return pl.pallas_call(
        _gmlp_kernel,
        grid=(T // tm,),
        in_specs=[pl.BlockSpec((tm, D), lambda i: (i, 0)), _const_spec(g.shape),
                  _const_spec(w_in.shape), _const_spec(b_in.shape), _const_spec(v_gain.shape),
                  _const_spec(w_s.shape), _const_spec(b_s_t.shape), _const_spec(w_out.shape)],
        out_specs=pl.BlockSpec((tm, D), lambda i: (i, 0)),
        out_shape=jax.ShapeDtypeStruct((T, D), F32),
        compiler_params=_cparams(("parallel",)),
        name="gmlp_mixer",
    )(x, g, w_in, b_in, v_gain, w_s, b_s_t, w_out)


def _mem_kv_kernel(m_ref, g_ref, w_ref, kg_ref, k_ref, v_ref):
    D = m_ref.shape[1]
    hd = D // X_HEADS
    mn = _rms(m_ref[...], g_ref[...]).astype(BF16)
    kv = _dot(mn, w_ref[...])
    v_ref[...] = kv[:, D:2 * D].astype(BF16)
    for h in range(X_HEADS):
        k_ref[:, h * hd:(h + 1) * hd] = _rms(kv[:, h * hd:(h + 1) * hd], kg_ref[...]).astype(BF16)


def _mem_kv(mem, g, wkv, k_gain, *, mem_len):
    M, D = mem.shape
    row = pl.BlockSpec((mem_len, D), lambda b: (b, 0))
    return pl.pallas_call(
        _mem_kv_kernel,
        grid=(M // mem_len,),
        in_specs=[row, _const_spec(g.shape), _const_spec(wkv.shape), _const_spec(k_gain.shape)],
        out_specs=[row, row],
        out_shape=[jax.ShapeDtypeStruct((M, D), BF16), jax.ShapeDtypeStruct((M, D), BF16)],
        compiler_params=_cparams(("parallel",)),
        name="mem_kv",
    )(mem, g, wkv, k_gain)


def _cross_router_kernel(x_ref, gx_ref, wq_ref, qg_ref, k_ref, v_ref, wo_ref, gf_ref,
                         wrh_ref, wrm_ref, br_ref,
                         x2_ref, xn_ref, slab_ref, slabt_ref, cnt_ref, carry_ref):
    i = pl.program_id(0)
    tm, D = x_ref.shape
    hd = D // X_HEADS

    @pl.when(i == 0)
    def _():
        carry_ref[...] = jnp.zeros_like(carry_ref)

    x = x_ref[...]
    xn = _rms(x, gx_ref[...]).astype(BF16)
    q = _dot(xn, wq_ref[...])
    scale = hd ** -0.5
    outs = []
    for h in range(X_HEADS):
        qh = _rms(q[:, h * hd:(h + 1) * hd], qg_ref[...]) * scale
        s = _dot_nt(qh.astype(BF16), k_ref[:, h * hd:(h + 1) * hd])
        m = jnp.max(s, axis=1, keepdims=True)
        p = jnp.exp(s - m)
        p = p / jnp.sum(p, axis=1, keepdims=True)
        outs.append(_dot(p.astype(BF16), v_ref[:, h * hd:(h + 1) * hd]))
    o = jnp.concatenate(outs, axis=1).astype(BF16)
    x2 = x + _dot(o, wo_ref[...])
    x2_ref[...] = x2

    xf = _rms(x2, gf_ref[...])
    _store_tok_tiles(xn_ref, xf)
    xh = xf.astype(BF16)
    xm = (xf - xh.astype(F32)).astype(BF16)
    wh = wrh_ref[...]
    logits = _dot(xh, wh) + (_dot(xh, wrm_ref[...]) + _dot(xm, wh)) + br_ref[...]

    lane = lax.broadcasted_iota(jnp.int32, logits.shape, 1).astype(F32)
    big = float(LANES)
    ninf = -jnp.inf
    glog = jnp.where(lane < N_GROUPS, logits, ninf)
    gmax = jnp.max(glog, axis=1, keepdims=True)
    gexp = jnp.exp(glog - gmax)
    gprob = gexp / jnp.sum(gexp, axis=1, keepdims=True)
    g_w = jnp.max(gprob, axis=1, keepdims=True)
    g_idx = jnp.min(jnp.where(gprob == g_w, lane, big), axis=1, keepdims=True)
    lo = N_GROUPS + EXPERTS_PER_GROUP * g_idx
    sel = jnp.logical_and(lane >= lo, lane < lo + EXPERTS_PER_GROUP)
    es = jnp.where(sel, logits, ninf)
    v1 = jnp.max(es, axis=1, keepdims=True)
    i1 = jnp.min(jnp.where(es == v1, lane, big), axis=1, keepdims=True)
    es2 = jnp.where(lane == i1, ninf, es)
    v2 = jnp.max(es2, axis=1, keepdims=True)
    i2 = jnp.min(jnp.where(es2 == v2, lane, big), axis=1, keepdims=True)
    t = jnp.exp(v2 - v1)
    w0 = g_w * (1.0 / (1.0 + t))
    w1 = g_w * (t / (1.0 + t))
    e0 = i1 - N_GROUPS
    e1 = i2 - N_GROUPS

    oh0 = lane == e0
    oh1 = lane == e1
    oh = jnp.where(jnp.logical_or(oh0, oh1), 1.0, 0.0)
    rr = lax.broadcasted_iota(jnp.int32, (tm, tm), 0)
    cc = lax.broadcasted_iota(jnp.int32, (tm, tm), 1)
    tri = jnp.where(cc < rr, 1.0, 0.0).astype(BF16)
    carry = carry_ref[...]
    before = _dot(tri, oh.astype(BF16)) + carry
    r0 = jnp.sum(jnp.where(oh0, before, 0.0), axis=1, keepdims=True)
    r1 = jnp.sum(jnp.where(oh1, before, 0.0), axis=1, keepdims=True)
    carry = carry + jnp.sum(oh, axis=0, keepdims=True)
    carry_ref[...] = carry
    cnt_ref[...] = jnp.broadcast_to(carry, cnt_ref.shape)

    slab = jnp.where(lane == 0, e0,
           jnp.where(lane == 1, e1,
           jnp.where(lane == 2, r0,
           jnp.where(lane == 3, r1,
           jnp.where(lane == 4, w0,
           jnp.where(lane == 5, w1, 0.0))))))
    slab_ref[...] = slab
    slabt_ref[...] = slab.T[0:8, :]


def _cross_router(x, gx, wq, q_gain, kmem, vmem, wo, gf, wr_hi, wr_mid, br, *, seq, mem_len, tm):
    T, D = x.shape
    tiles_per_seq = seq // tm
    row = lambda w: pl.BlockSpec((tm, w), lambda i: (i, 0))
    memspec = pl.BlockSpec((mem_len, D), lambda i: (i // tiles_per_seq, 0))
    return pl.pallas_call(
        _cross_router_kernel,
        grid=(T // tm,),
        in_specs=[row(D), _const_spec(gx.shape), _const_spec(wq.shape), _const_spec(q_gain.shape),
                  memspec, memspec, _const_spec(wo.shape), _const_spec(gf.shape),
                  _const_spec(wr_hi.shape), _const_spec(wr_mid.shape), _const_spec(br.shape)],
        out_specs=[row(D), pl.BlockSpec((tm * ROW_TILE, LANES), lambda i: (i, 0)), row(LANES),
                   pl.BlockSpec((8, tm), lambda i: (0, i)), _const_spec((8, LANES))],
        out_shape=[jax.ShapeDtypeStruct((T, D), F32), jax.ShapeDtypeStruct((T * ROW_TILE, LANES), F32),
                   jax.ShapeDtypeStruct((T, LANES), F32), jax.ShapeDtypeStruct((8, T), F32),
                   jax.ShapeDtypeStruct((8, LANES), F32)],
        scratch_shapes=[pltpu.VMEM((1, LANES), F32)],
        compiler_params=_cparams(("arbitrary",)),
        name="cross_attn_router",
    )(x, gx, wq, q_gain, kmem, vmem, wo, gf, wr_hi, wr_mid, br)


def _for_rows(n, fn):
    groups = lax.shift_right_logical(n, 3)

    def body8(g, carry):
        for u in range(8):
            fn(g * 8 + u)
        return carry
    lax.fori_loop(0, groups, body8, 0)

    def body1(r, carry):
        fn(r)
        return carry
    lax.fori_loop(groups * 8, n, body1, 0)


def _expert_kernel(te_ref, nr_ref, nv_ref, tok_ref, tokn_ref, x_hbm, wg_ref, wu_ref, wd_ref, y_ref,
                   xbuf, sem):
    t = pl.program_id(0)
    tm = y_ref.shape[0] // ROW_TILE
    n_valid = nv_ref[0]
    slot = t % 2

    def row_copy(src_row, dst_slot, r):
        return pltpu.make_async_copy(x_hbm.at[pl.ds(src_row * ROW_TILE, ROW_TILE)],
                                     xbuf.at[dst_slot, pl.ds(r * ROW_TILE, ROW_TILE)], sem.at[dst_slot])

    def issue(idx_ref, dst_slot, n):
        _for_rows(n, lambda r: row_copy(idx_ref[0, 0, r], dst_slot, r).start())

    @pl.when(t == 0)
    def _():
        xbuf[...] = jnp.zeros_like(xbuf)

    @pl.when(jnp.logical_and(t == 0, n_valid > 0))
    def _():
        issue(tok_ref, 0, nr_ref[0])

    @pl.when(t + 1 < n_valid)
    def _():
        issue(tokn_ref, 1 - slot, nr_ref[t + 1])

    @pl.when(t < n_valid)
    def _():
        _for_rows(nr_ref[t], lambda r: row_copy(0, slot, r).wait())
        x = jnp.concatenate([_load_tok_cols(xbuf, slot, 0, tm, j) for j in range(ROW_TILE)],
                            axis=1).astype(BF16)
        g = _dot(x, wg_ref[0, 0].astype(BF16))
        u = _dot(x, wu_ref[0, 0].astype(BF16))
        h = (g * _sigmoid(g) * u).astype(BF16)
        _store_tok_tiles(y_ref, _dot(h, wd_ref[0, 0].astype(BF16)))

    @pl.when(t >= n_valid)
    def _():
        y_ref[...] = jnp.zeros_like(y_ref)


def _experts(tile_expert, tile_rows, n_valid, tok_sorted, xn3, w_gate, w_up, w_down, *, layer, tm):
    n_tiles = tile_expert.shape[0]
    D = ROW_TILE * LANES
    de = w_gate.shape[3]
    tok3 = tok_sorted.reshape(n_tiles, 1, tm)
    grid_spec = pltpu.PrefetchScalarGridSpec(
        num_scalar_prefetch=3,
        grid=(n_tiles,),
        in_specs=[
            pl.BlockSpec((1, 1, tm), lambda t, te, nr, nv: (t, 0, 0), memory_space=pltpu.SMEM),
            pl.BlockSpec((1, 1, tm), lambda t, te, nr, nv: (jnp.minimum(t + 1, n_tiles - 1), 0, 0),
                         memory_space=pltpu.SMEM),
            pl.BlockSpec(memory_space=pl.ANY),
            pl.BlockSpec((1, 1, D, de), lambda t, te, nr, nv: (layer, te[t], 0, 0)),
            pl.BlockSpec((1, 1, D, de), lambda t, te, nr, nv: (layer, te[t], 0, 0)),
            pl.BlockSpec((1, 1, de, D), lambda t, te, nr, nv: (layer, te[t], 0, 0)),
        ],
        out_specs=pl.BlockSpec((tm * ROW_TILE, LANES), lambda t, te, nr, nv: (t, 0)),
        scratch_shapes=[pltpu.VMEM((2, tm * ROW_TILE, LANES), F32), pltpu.SemaphoreType.DMA((2,))],
    )
    return pl.pallas_call(
        _expert_kernel,
        grid_spec=grid_spec,
        out_shape=jax.ShapeDtypeStruct((n_tiles * tm * ROW_TILE, LANES), F32),
        compiler_params=_cparams(("arbitrary",)),
        name="moe_experts",
    )(tile_expert, tile_rows, n_valid, tok3, tok3, xn3, w_gate, w_up, w_down)


def _invperm_kernel(lo_ref, hi_ref, pos_ref, tok_ref, *, n_tok):
    i = pl.program_id(0)
    chunk = pos_ref.shape[2]

    @pl.when(i == 0)
    def _():
        def per_range(e, carry):
            def zero(p, c):
                tok_ref[p] = 0
                return c
            lax.fori_loop(lo_ref[e], hi_ref[e], zero, 0)
            return carry
        lax.fori_loop(0, lo_ref.shape[0], per_range, 0)

    tbase = lax.rem(i * chunk, n_tok)

    def body(s, carry):
        tok_ref[pos_ref[0, 0, s]] = tbase + s
        return carry
    lax.fori_loop(0, chunk, body, 0, unroll=8)


def _invperm(pad_lo, pad_hi, pos_flat, *, n_pos, n_tok, chunk):
    n = pos_flat.shape[0] // chunk
    grid_spec = pltpu.PrefetchScalarGridSpec(
        num_scalar_prefetch=2,
        grid=(n,),
        in_specs=[pl.BlockSpec((1, 1, chunk), lambda i, lo, hi: (i, 0, 0), memory_space=pltpu.SMEM)],
        out_specs=pl.BlockSpec(memory_space=pltpu.SMEM),
    )
    return pl.pallas_call(
        functools.partial(_invperm_kernel, n_tok=n_tok),
        grid_spec=grid_spec,
        out_shape=jax.ShapeDtypeStruct((n_pos,), jnp.int32),
        compiler_params=_cparams(("arbitrary",)),
        name="moe_invperm",
    )(pad_lo, pad_hi, pos_flat.reshape(n, 1, chunk))


def _combine_kernel(pos_ref, posn_ref, x_ref, slab_ref, y_hbm, o_ref, ybuf, sem):
    i = pl.program_id(0)
    n = pl.num_programs(0)
    tc = x_ref.shape[0]
    slot = i % 2

    def row_copy(src_row, dst_slot, r):
        return pltpu.make_async_copy(y_hbm.at[pl.ds(src_row * ROW_TILE, ROW_TILE)],
                                     ybuf.at[dst_slot, pl.ds(r * ROW_TILE, ROW_TILE)], sem.at[dst_slot])

    def issue(idx_ref, dst_slot):
        def body(r, carry):
            row_copy(idx_ref[0, 0, r], dst_slot, r).start()
            return carry
        lax.fori_loop(0, 2 * tc, body, 0, unroll=8)

    @pl.when(i == 0)
    def _():
        issue(pos_ref, 0)

    @pl.when(i + 1 < n)
    def _():
        issue(posn_ref, 1 - slot)

    def wbody(r, carry):
        row_copy(0, slot, r).wait()
        return carry
    lax.fori_loop(0, 2 * tc, wbody, 0, unroll=8)

    slab = slab_ref[...]
    w0 = slab[:, 4:5]
    w1 = slab[:, 5:6]
    w0 = jnp.broadcast_to(w0, (tc, LANES))
    w1 = jnp.broadcast_to(w1, (tc, LANES))
    for j in range(ROW_TILE):
        cols = slice(j * LANES, (j + 1) * LANES)
        o_ref[:, cols] = x_ref[:, cols] + (w0 * _load_tok_cols(ybuf, slot, 0, tc, j)
                                           + w1 * _load_tok_cols(ybuf, slot, tc, tc, j))


def _combine(pos3, x, slab, y, *, tc):
    T, D = x.shape
    n = T // tc
    return pl.pallas_call(
        _combine_kernel,
        grid=(n,),
        in_specs=[
            pl.BlockSpec((1, 1, 2 * tc), lambda i: (i, 0, 0), memory_space=pltpu.SMEM),
            pl.BlockSpec((1, 1, 2 * tc), lambda i: (jnp.minimum(i + 1, n - 1), 0, 0),
                         memory_space=pltpu.SMEM),
            pl.BlockSpec((tc, D), lambda i: (i, 0)),
            pl.BlockSpec((tc, LANES), lambda i: (i, 0)),
            pl.BlockSpec(memory_space=pl.ANY),
        ],
        out_specs=pl.BlockSpec((tc, D), lambda i: (i, 0)),
        out_shape=jax.ShapeDtypeStruct((T, D), F32),
        scratch_shapes=[pltpu.VMEM((2, 2 * tc * ROW_TILE, LANES), F32), pltpu.SemaphoreType.DMA((2,))],
        compiler_params=_cparams(("arbitrary",)),
        name="moe_combine",
    )(pos3, pos3, x, slab, y)


def _row(v):
    return v.reshape(1, -1).astype(F32)


def _pack_ab_weights(w_in, f_bias, q_gain, k_gain):
    D = w_in.shape[0]
    wq = w_in[:, 0:A_WIDTH].reshape(D, A_HEADS, HEAD_DIM)
    wk = w_in[:, A_WIDTH:2 * A_WIDTH].reshape(D, A_HEADS, HEAD_DIM)
    wqk = jnp.stack([wq, wk], axis=2).reshape(D, 2 * A_WIDTH)
    wv = w_in[:, 2 * A_WIDTH:3 * A_WIDTH]
    o = 3 * A_WIDTH
    wf = w_in[:, o:o + A_HEADS]
    wf3 = jnp.concatenate([wf, wf, wf, jnp.zeros((D, LANES - 3 * A_HEADS), F32)], axis=1)
    wxb = w_in[:, o + A_HEADS:o + A_HEADS + B_WIDTH]
    wgb = w_in[:, o + A_HEADS + B_WIDTH:o + A_HEADS + 2 * B_WIDTH]
    w_cat = jnp.concatenate([wqk, wv, wf3, wxb, wgb], axis=1).astype(BF16)
    fb3 = jnp.concatenate([f_bias, f_bias, f_bias, jnp.zeros((LANES - 3 * A_HEADS,), F32)]).reshape(1, LANES)
    gain_qk = jnp.concatenate([q_gain * (HEAD_DIM ** -0.5 * LOG2E), k_gain]).reshape(1, LANES)
    return w_cat, fb3, gain_qk


def _decay_placement():
    peq = [[0.0] * (A_HEADS * LANES) for _ in range(LANES)]
    pek = [[0.0] * (A_HEADS * LANES) for _ in range(LANES)]
    for h in range(A_HEADS):
        base = h * LANES + HEAD_DIM
        for part in range(3):
            peq[part * 8 + h][base + part] = 1.0
            peq[24][base + 3 + part] = 1.0
            pek[24][base + part] = 1.0
            pek[part * 8 + h][base + 3 + part] = -1.0
    gm = [[(1.0 / HEAD_DIM) if (r // HEAD_DIM) == (c // HEAD_DIM) else 0.0 for c in range(LANES)]
          for r in range(LANES)]
    return (jnp.array(peq, BF16), jnp.array(pek, BF16), jnp.array(gm, BF16))


def _block_diag(w):
    nb, bi, bo = w.shape
    eye = jnp.eye(nb, dtype=w.dtype)
    return (eye[:, None, :, None] * w[:, :, None, :]).reshape(nb * bi, nb * bo)


def _routing(slabt, counts, *, tm, n_tiles):
    e = slabt[0:2].astype(jnp.int32)
    r = slabt[2:4].astype(jnp.int32)
    cnt = counts[0, :N_EXPERTS].astype(jnp.int32)
    tiles_e = (cnt + tm - 1) // tm
    tile_end = jnp.cumsum(tiles_e)
    tile_start = tile_end - tiles_e
    pos = jnp.take(tile_start * tm, e) + r
    n_valid = tile_end[-1].astype(jnp.int32)
    tiles = jnp.arange(n_tiles, dtype=jnp.int32)
    te = jnp.sum((tiles[:, None] >= tile_end[None, :]).astype(jnp.int32), axis=1)
    te = jnp.minimum(te, N_EXPERTS - 1)
    rows = jnp.clip(jnp.take(cnt, te) - (tiles - jnp.take(tile_start, te)) * tm, 0, tm)
    valid = tiles < n_valid
    te = jnp.where(valid, te, jnp.take(te, jnp.maximum(n_valid - 1, 0)))
    rows = jnp.where(valid, rows, 0).astype(jnp.int32)
    end_all = jnp.full((1,), n_tiles * tm, jnp.int32)
    pad_lo = jnp.concatenate([tile_start * tm + cnt, tile_end[-1:] * tm]).astype(jnp.int32)
    pad_hi = jnp.concatenate([tile_end * tm, end_all]).astype(jnp.int32)
    return pos, te.astype(jnp.int32), rows, n_valid.reshape(1), pad_lo, pad_hi


def _moe(x2, xn3, slab, slabt, counts, w_gate, w_up, w_down, *, layer, tm_e, tc):
    T = x2.shape[0]
    n_tiles = (2 * T) // tm_e + N_EXPERTS
    pos, te, rows, n_valid, pad_lo, pad_hi = _routing(slabt, counts, tm=tm_e, n_tiles=n_tiles)
    tok_sorted = _invperm(pad_lo, pad_hi, pos.reshape(-1), n_pos=n_tiles * tm_e, n_tok=T,
                          chunk=min(8192, T))
    y3 = _experts(te, rows, n_valid, tok_sorted, xn3, w_gate, w_up, w_down, layer=layer, tm=tm_e)
    pos3 = pos.reshape(2, T // tc, tc).transpose(1, 0, 2).reshape(T // tc, 1, 2 * tc)
    return _combine(pos3, x2, slab, y3, tc=tc)


def _cross_and_moe(x, mem2, layer, norm_cross, norm_mem, norm_ffn, x_wq, x_wkv, x_wo, x_q_gain,
                   x_k_gain, moe_wg, moe_bg, moe_we, moe_be, moe_w_gate, moe_w_up, moe_w_down,
                   *, seq, mem_len, tm, tm_e, tc):
    kmem, vmem = _mem_kv(mem2, _row(norm_mem[layer]), x_wkv[layer].astype(BF16),
                         _row(x_k_gain[layer]), mem_len=mem_len)
    D = x.shape[1]
    wr = jnp.concatenate([moe_wg[layer], moe_we[layer],
                          jnp.zeros((D, LANES - N_GROUPS - N_EXPERTS), F32)], axis=1)
    wr_hi = wr.astype(BF16)
    wr_mid = (wr - wr_hi.astype(F32)).astype(BF16)
    br = jnp.concatenate([moe_bg[layer], moe_be[layer],
                          jnp.zeros((LANES - N_GROUPS - N_EXPERTS,), F32)]).reshape(1, LANES)
    x2, xn3, slab, slabt, counts = _cross_router(
        x, _row(norm_cross[layer]), x_wq[layer].astype(BF16), _row(x_q_gain[layer]), kmem, vmem,
        x_wo[layer].astype(BF16), _row(norm_ffn[layer]), wr_hi, wr_mid, br,
        seq=seq, mem_len=mem_len, tm=tm)
    return _moe(x2, xn3, slab, slabt, counts, moe_w_gate, moe_w_up, moe_w_down,
                layer=layer, tm_e=tm_e, tc=tc)


def kernel(x, mem, norm_mix, norm_cross, norm_mem, norm_ffn, ab_w_in, ab_f_bias, ab_q_gain, ab_k_gain, ab_conv_w, ab_conv_b, ab_wa, ab_ba, ab_wx, ab_bx, ab_lambda, ab_w_out, c_w_in, c_b_in, c_v_gain, c_w_s, c_b_s, c_w_out, x_wq, x_wkv, x_wo, x_q_gain, x_k_gain, moe_wg, moe_bg, moe_we, moe_be, moe_w_gate, moe_w_up, moe_w_down):
    B, S, D = x.shape
    mem_len = mem.shape[1]
    T = B * S
    tm = min(512, S)
    x2d = x.reshape(T, D)
    mem2 = mem.reshape(B * mem_len, D)
    common = dict(seq=S, mem_len=mem_len, tm=tm, tm_e=min(512, T), tc=min(256, T))
    tail = (norm_cross, norm_mem, norm_ffn, x_wq, x_wkv, x_wo, x_q_gain, x_k_gain,
            moe_wg, moe_bg, moe_we, moe_be, moe_w_gate, moe_w_up, moe_w_down)

    w_cat, fb3, gain_qk = _pack_ab_weights(ab_w_in[0], ab_f_bias[0], ab_q_gain[0], ab_k_gain[0])
    peq, pek, gmat = _decay_placement()
    qp, kp, v, xb, gb = _in_proj(x2d, _row(norm_mix[0]), w_cat, fb3, gain_qk, gmat, peq, pek,
                                 seq=S, tm=tm)
    attn = _fox_attention(qp, kp, v, batch=B, seq=S, tq=tm)
    w_gates = jnp.concatenate([_block_diag(ab_wa[0]), _block_diag(ab_wx[0])], axis=1).astype(BF16)
    b_gates = jnp.concatenate([ab_ba[0], ab_bx[0]]).reshape(1, -1)
    rec = _rglru(xb, gb, ab_conv_w[0], _row(ab_conv_b[0]), w_gates, b_gates, _row(ab_lambda[0]),
                 batch=B, seq=S, tt=min(256, S))
    x2d = _out_proj(x2d, attn, rec, ab_w_out[0].astype(BF16), tm=tm)
    x2d = _cross_and_moe(x2d, mem2, 0, *tail, **common)

    x2d = _gmlp(x2d, _row(norm_mix[1]), c_w_in[0].astype(BF16), _row(c_b_in[0]), _row(c_v_gain[0]),
                c_w_s[0], c_b_s[0].T, c_w_out[0].astype(BF16), tm=min(256, S))
    x2d = _cross_and_moe(x2d, mem2, 1, *tail, **common)
    return x2d.reshape(B, S, D)
```

```python
import functools
import math

import jax
import jax.numpy as jnp
from jax import lax
from jax.experimental import pallas as pl
from jax.experimental.pallas import tpu as pltpu

EPS = 1e-6
A_HEADS = 8
HEAD_DIM = 64
A_WIDTH = A_HEADS * HEAD_DIM
B_WIDTH = 512
B_BLOCKS = 8
CONV_WIDTH = 4
RG_C = 8.0
C_GROUPS = 8
C_CHUNK = 128
X_HEADS = 4
N_GROUPS = 4
EXPERTS_PER_GROUP = 8
N_EXPERTS = N_GROUPS * EXPERTS_PER_GROUP

LOG2E = 1.4426950408889634
LANES = 128
VMEM_LIMIT = 56 * 1024 * 1024

F32 = jnp.float32
BF16 = jnp.bfloat16


def _cparams(sem, vmem=VMEM_LIMIT):
    return pltpu.CompilerParams(dimension_semantics=sem, vmem_limit_bytes=vmem)


def _const_spec(shape):
    nd = len(shape)
    return pl.BlockSpec(shape, lambda *_: (0,) * nd)


def _rms(x, g):
    ms = jnp.mean(x * x, axis=-1, keepdims=True)
    return x * lax.rsqrt(ms + EPS) * g


def _gelu(x):
    c = math.sqrt(2.0 / math.pi)
    return 0.5 * x * (1.0 + jnp.tanh(c * (x + 0.044715 * (x * x * x))))


def _sigmoid(x):
    return 1.0 / (1.0 + jnp.exp(-x))


def _softplus(x):
    return jnp.maximum(x, 0.0) + jnp.log1p(jnp.exp(-jnp.abs(x)))


def _dot(a, b):
    return jnp.dot(a, b, preferred_element_type=F32)


def _dot_nt(a, b):
    return lax.dot_general(a, b, (((1,), (1,)), ((), ())), preferred_element_type=F32)


def _split3(x):
    hi = x.astype(BF16).astype(F32)
    r = x - hi
    mid = r.astype(BF16).astype(F32)
    lo = (r - mid).astype(BF16).astype(F32)
    return hi, mid, lo


ROW_TILE = 8


def _store_tok_tiles(ref, val):
    m = val.shape[0]
    for j in range(ROW_TILE):
        ref[pl.ds(j, m, stride=ROW_TILE), :] = val[:, j * LANES:(j + 1) * LANES]


def _load_tok_cols(ref, slot, lo, n, j):
    return ref[slot, pl.ds(lo * ROW_TILE + j, n, stride=ROW_TILE), :]


def _in_proj_kernel(x_ref, g_ref, w_ref, fb_ref, gain_ref, gmat_ref, peq_ref, pek_ref,
                    qp_ref, kp_ref, v_ref, xb_ref, gb_ref, carry_ref, *, tiles_per_seq):
    i = pl.program_id(0)
    tm = x_ref.shape[0]

    @pl.when(i % tiles_per_seq == 0)
    def _():
        carry_ref[...] = jnp.zeros_like(carry_ref)

    xn = _rms(x_ref[...], g_ref[...]).astype(BF16)
    proj = _dot(xn, w_ref[...])
    qk = proj[:, 0:1024]
    v_ref[...] = proj[:, 1024:1536].astype(BF16)
    fl = proj[:, 1536:1664]
    xb_ref[...] = proj[:, 1664:2176]
    gb_ref[...] = proj[:, 2176:2688]

    logf = -_softplus(-(fl + fb_ref[...]))
    row = lax.broadcasted_iota(jnp.int32, logf.shape, 0)
    c = logf
    s = 1
    while s < tm:
        c = c + jnp.where(row >= s, pltpu.roll(c, s, axis=0), 0.0)
        s *= 2
    c = c + carry_ref[...]
    carry_ref[...] = c[tm - 1:tm, :]

    hi, mid, lo = _split3(c * LOG2E)
    lane = lax.broadcasted_iota(jnp.int32, c.shape, 1)
    cpack = jnp.where(lane < 8, hi, jnp.where(lane < 16, mid, jnp.where(lane < 24, lo,
                      jnp.where(lane == 24, 1.0, 0.0)))).astype(BF16)
    ext_q = _dot(cpack, peq_ref[...])
    ext_k = _dot(cpack, pek_ref[...])

    gmat = gmat_ref[...]
    gain = gain_ref[...]
    lane_b = lax.broadcasted_iota(jnp.int32, (tm, LANES), 1)
    for hp in range(A_HEADS // 2):
        pair = qk[:, hp * 2 * LANES:(hp + 1) * 2 * LANES]
        ms2 = _dot((pair * pair).astype(BF16), gmat)
        for hh in range(2):
            h = 2 * hp + hh
            blk = pair[:, hh * LANES:(hh + 1) * LANES]
            nb = blk * lax.rsqrt(ms2[:, hh * LANES:(hh + 1) * LANES] + EPS) * gain
            qp = jnp.where(lane_b < HEAD_DIM, nb, ext_q[:, h * LANES:(h + 1) * LANES])
            kp = jnp.where(lane_b < HEAD_DIM, pltpu.roll(nb, HEAD_DIM, axis=1),
                           ext_k[:, h * LANES:(h + 1) * LANES])
            qp_ref[:, h * LANES:(h + 1) * LANES] = qp.astype(BF16)
            kp_ref[:, h * LANES:(h + 1) * LANES] = kp.astype(BF16)


def _in_proj(x, g, w_cat, fb3, gain_qk, gmat, peq, pek, *, seq, tm):
    T, D = x.shape
    n = T // tm
    row = lambda w: pl.BlockSpec((tm, w), lambda i: (i, 0))
    return pl.pallas_call(
        functools.partial(_in_proj_kernel, tiles_per_seq=seq // tm),
        grid=(n,),
        in_specs=[row(D), _const_spec(g.shape), _const_spec(w_cat.shape), _const_spec(fb3.shape),
                  _const_spec(gain_qk.shape), _const_spec(gmat.shape), _const_spec(peq.shape),
                  _const_spec(pek.shape)],
        out_specs=[row(1024), row(1024), row(512), row(512), row(512)],
        out_shape=[jax.ShapeDtypeStruct((T, 1024), BF16), jax.ShapeDtypeStruct((T, 1024), BF16),
                   jax.ShapeDtypeStruct((T, 512), BF16), jax.ShapeDtypeStruct((T, 512), F32),
                   jax.ShapeDtypeStruct((T, 512), F32)],
        scratch_shapes=[pltpu.VMEM((1, LANES), F32)],
        compiler_params=_cparams(("arbitrary",)),
        name="in_proj_ab",
    )(x, g, w_cat, fb3, gain_qk, gmat, peq, pek)


def _fox_kernel(q_ref, k_ref, v_ref, o_ref, m0, l0, a0, m1, l1, a1):
    i = pl.program_id(2)
    tq = q_ref.shape[0]
    stats = ((m0, l0, a0), (m1, l1, a1))
    for m_ref, l_ref, a_ref in stats:
        m_ref[...] = jnp.full_like(m_ref, -jnp.inf)
        l_ref[...] = jnp.zeros_like(l_ref)
        a_ref[...] = jnp.zeros_like(a_ref)

    def tile(j, masked):
        rows = pl.ds(pl.multiple_of(j * tq, tq), tq)
        vpair = v_ref[rows, :]
        for hh in range(2):
            m_ref, l_ref, a_ref = stats[hh]
            q = q_ref[:, hh * LANES:(hh + 1) * LANES]
            k = k_ref[rows, hh * LANES:(hh + 1) * LANES]
            s = _dot_nt(q, k)
            if masked:
                r = lax.broadcasted_iota(jnp.int32, (tq, tq), 0)
                c = lax.broadcasted_iota(jnp.int32, (tq, tq), 1)
                s = jnp.where(c <= r, s, -jnp.inf)
            m_prev = m_ref[...]
            m_new = jnp.maximum(m_prev, jnp.max(s, axis=1, keepdims=True))
            alpha = jnp.exp2(m_prev - m_new)
            p = jnp.exp2(s - jnp.concatenate([m_new] * (tq // LANES), axis=1))
            l_ref[...] = alpha * l_ref[...] + jnp.sum(p, axis=1, keepdims=True)
            a_ref[...] = alpha * a_ref[...] + _dot(p.astype(BF16), vpair)
            m_ref[...] = m_new

    def body(jj, carry):
        tile(2 * jj, False)
        tile(2 * jj + 1, False)
        return carry
    lax.fori_loop(0, lax.shift_right_logical(i, 1), body, 0)

    @pl.when(i % 2 == 1)
    def _():
        tile(i - 1, False)
    tile(i, True)
    lane = lax.broadcasted_iota(jnp.int32, (tq, LANES), 1)
    o_ref[...] = jnp.where(lane < HEAD_DIM, a0[...] / l0[...], a1[...] / l1[...]).astype(o_ref.dtype)


def _fox_attention(qp, kp, v, *, batch, seq, tq):
    T = batch * seq
    nq = seq // tq
    stat = pltpu.VMEM((tq, LANES), F32)
    return pl.pallas_call(
        _fox_kernel,
        grid=(batch, A_HEADS // 2, nq),
        in_specs=[
            pl.BlockSpec((tq, 2 * LANES), lambda b, hp, i: (b * nq + i, hp)),
            pl.BlockSpec((seq, 2 * LANES), lambda b, hp, i: (b, hp)),
            pl.BlockSpec((seq, LANES), lambda b, hp, i: (b, hp)),
        ],
        out_specs=pl.BlockSpec((tq, LANES), lambda b, hp, i: (b * nq + i, hp)),
        out_shape=jax.ShapeDtypeStruct((T, A_WIDTH), BF16),
        scratch_shapes=[stat] * 6,
        compiler_params=_cparams(("parallel", "parallel", "arbitrary")),
        name="fox_attention",
    )(qp, kp, v)


def _rglru_kernel(xb_ref, gb_ref, cw_ref, cb_ref, wg_ref, bg_ref, lam_ref, o_ref,
                  xs_ref, h_ref):
    i = pl.program_id(1)
    tt = xb_ref.shape[0]
    pad = 8

    @pl.when(i == 0)
    def _():
        xs_ref[0:pad, :] = jnp.zeros((pad, B_WIDTH), F32)
        h_ref[...] = jnp.zeros_like(h_ref)

    xs_ref[pad:pad + tt, :] = xb_ref[...]
    xc = cb_ref[...]
    for k in range(CONV_WIDTH):
        off = pad - (CONV_WIDTH - 1) + k
        xc = xc + cw_ref[k:k + 1, :] * xs_ref[off:off + tt, :]
    xs_ref[0:pad, :] = xs_ref[tt:tt + pad, :]

    pre = _dot(xc.astype(BF16), wg_ref[...]) + bg_ref[...]
    r = _sigmoid(pre[:, 0:B_WIDTH])
    ig = _sigmoid(pre[:, B_WIDTH:2 * B_WIDTH])
    log_a = (-RG_C) * r * _softplus(-lam_ref[...])
    a = jnp.exp(log_a)
    th = jnp.tanh(log_a)
    beta = jnp.sqrt(-2.0 * th / (1.0 - th))
    bv = beta * (ig * xc)

    groups = tt // ROW_TILE
    a3 = a.reshape(groups, ROW_TILE, B_WIDTH)
    b3 = bv.reshape(groups, ROW_TILE, B_WIDTH)
    sub = lax.broadcasted_iota(jnp.int32, a3.shape, 1)
    s = 1
    while s < ROW_TILE:
        keep = sub >= s
        a_sh = jnp.where(keep, pltpu.roll(a3, s, axis=1), 1.0)
        b_sh = jnp.where(keep, pltpu.roll(b3, s, axis=1), 0.0)
        b3 = a3 * b_sh + b3
        a3 = a3 * a_sh
        s *= 2
    h_prev = h_ref[...]
    hs = []
    for g in range(groups):
        hg = b3[g] + a3[g] * h_prev
        hs.append(hg)
        h_prev = hg[ROW_TILE - 1:ROW_TILE, :]
    h_ref[...] = h_prev
    h = jnp.concatenate(hs, axis=0)
    o_ref[...] = (_gelu(gb_ref[...]) * h).astype(o_ref.dtype)


def _rglru(xb, gb, conv_w, conv_b, w_gates, b_gates, lam, *, batch, seq, tt):
    T = batch * seq
    nt = seq // tt
    row = pl.BlockSpec((tt, B_WIDTH), lambda b, i: (b * nt + i, 0))
    return pl.pallas_call(
        _rglru_kernel,
        grid=(batch, nt),
        in_specs=[row, row, _const_spec(conv_w.shape), _const_spec(conv_b.shape),
                  _const_spec(w_gates.shape), _const_spec(b_gates.shape), _const_spec(lam.shape)],
        out_specs=row,
        out_shape=jax.ShapeDtypeStruct((T, B_WIDTH), BF16),
        scratch_shapes=[pltpu.VMEM((tt + 8, B_WIDTH), F32), pltpu.VMEM((1, B_WIDTH), F32)],
        compiler_params=_cparams(("parallel", "arbitrary")),
        name="rglru",
    )(xb, gb, conv_w, conv_b, w_gates, b_gates, lam)


def _out_proj_kernel(x_ref, a_ref, r_ref, w_ref, o_ref):
    o_ref[...] = (x_ref[...] + _dot(a_ref[...], w_ref[0:A_WIDTH, :])
                  + _dot(r_ref[...], w_ref[A_WIDTH:A_WIDTH + B_WIDTH, :]))


def _out_proj(x, attn, rec, w_out, *, tm):
    T, D = x.shape
    return pl.pallas_call(
        _out_proj_kernel,
        grid=(T // tm,),
        in_specs=[pl.BlockSpec((tm, D), lambda i: (i, 0)),
                  pl.BlockSpec((tm, A_WIDTH), lambda i: (i, 0)),
                  pl.BlockSpec((tm, B_WIDTH), lambda i: (i, 0)),
                  _const_spec(w_out.shape)],
        out_specs=pl.BlockSpec((tm, D), lambda i: (i, 0)),
        out_shape=jax.ShapeDtypeStruct((T, D), F32),
        compiler_params=_cparams(("parallel",)),
        name="out_proj_ab",
    )(x, attn, rec, w_out)


def _gmlp_kernel(x_ref, g_ref, wi_ref, bi_ref, vg_ref, ws_ref, bs_ref, wo_ref, o_ref):
    tm = x_ref.shape[0]
    cw = vg_ref.shape[1]
    gd = cw // C_GROUPS
    x = x_ref[...]
    xn = _rms(x, g_ref[...]).astype(BF16)

    u = _gelu(_dot(xn, wi_ref[:, 0:cw]) + bi_ref[:, 0:cw])
    v = _gelu(_dot(xn, wi_ref[:, cw:2 * cw]) + bi_ref[:, cw:2 * cw])
    vn = _rms(v, vg_ref[...]).astype(BF16)
    r = lax.broadcasted_iota(jnp.int32, (C_CHUNK, C_CHUNK), 0)
    c = lax.broadcasted_iota(jnp.int32, (C_CHUNK, C_CHUNK), 1)
    cols = []
    for g in range(C_GROUPS):
        wt = jnp.where(c <= r, ws_ref[g], 0.0).astype(BF16)
        bias = bs_ref[:, g:g + 1]
        mixed = jnp.concatenate(
            [_dot(wt, vn[ch * C_CHUNK:(ch + 1) * C_CHUNK, g * gd:(g + 1) * gd]) + bias
             for ch in range(tm // C_CHUNK)], axis=0)
        cols.append(u[:, g * gd:(g + 1) * gd] * mixed)
    y = jnp.concatenate(cols, axis=1).astype(BF16)
    o_ref[...] = x + _dot(y, wo_ref[...])


def _gmlp(x, g, w_in, b_in, v_gain, w_s, b_s_t, w_out, *, tm):
    T, D = x.shape
    return pl.pallas_call(
        _gmlp_kernel,
        grid=(T // tm,),
        in_specs=[pl.BlockSpec((tm, D), lambda i: (i, 0)), _const_spec(g.shape),
                  _const_spec(w_in.shape), _const_spec(b_in.shape), _const_spec(v_gain.shape),
                  _const_spec(w_s.shape), _const_spec(b_s_t.shape), _const_spec(w_out.shape)],
        out_specs=pl.BlockSpec((tm, D), lambda i: (i, 0)),
        out_shape=jax.ShapeDtypeStruct((T, D), F32),
        compiler_params=_cparams(("parallel",)),
        name="gmlp_mixer",
    )(x, g, w_in, b_in, v_gain, w_s, b_s_t, w_out)


def _mem_kv_kernel(m_ref, g_ref, w_ref, kg_ref, k_ref, v_ref):
    D = m_ref.shape[1]
    hd = D // X_HEADS
    mn = _rms(m_ref[...], g_ref[...]).astype(BF16)
    kv = _dot(mn, w_ref[...])
    v_ref[...] = kv[:, D:2 * D].astype(BF16)
    for h in range(X_HEADS):
        k_ref[:, h * hd:(h + 1) * hd] = _rms(kv[:, h * hd:(h + 1) * hd], kg_ref[...]).astype(BF16)


def _mem_kv(mem, g, wkv, k_gain, *, mem_len):
    M, D = mem.shape
    row = pl.BlockSpec((mem_len, D), lambda b: (b, 0))
    return pl.pallas_call(
        _mem_kv_kernel,
        grid=(M // mem_len,),
        in_specs=[row, _const_spec(g.shape), _const_spec(wkv.shape), _const_spec(k_gain.shape)],
        out_specs=[row, row],
        out_shape=[jax.ShapeDtypeStruct((M, D), BF16), jax.ShapeDtypeStruct((M, D), BF16)],
        compiler_params=_cparams(("parallel",)),
        name="mem_kv",
    )(mem, g, wkv, k_gain)


def _cross_router_kernel(x_ref, gx_ref, wq_ref, qg_ref, k_ref, v_ref, wo_ref, gf_ref,
                         wrh_ref, wrm_ref, br_ref,
                         x2_ref, xn_ref, slab_ref, slabt_ref, cnt_ref, carry_ref):
    i = pl.program_id(0)
    tm, D = x_ref.shape
    hd = D // X_HEADS

    @pl.when(i == 0)
    def _():
        carry_ref[...] = jnp.zeros_like(carry_ref)

    x = x_ref[...]
    xn = _rms(x, gx_ref[...]).astype(BF16)
    q = _dot(xn, wq_ref[...])
    scale = hd ** -0.5
    outs = []
    for h in range(X_HEADS):
        qh = _rms(q[:, h * hd:(h + 1) * hd], qg_ref[...]) * scale
        s = _dot_nt(qh.astype(BF16), k_ref[:, h * hd:(h + 1) * hd])
        m = jnp.max(s, axis=1, keepdims=True)
        p = jnp.exp(s - m)
        p = p / jnp.sum(p, axis=1, keepdims=True)
        outs.append(_dot(p.astype(BF16), v_ref[:, h * hd:(h + 1) * hd]))
    o = jnp.concatenate(outs, axis=1).astype(BF16)
    x2 = x + _dot(o, wo_ref[...])
    x2_ref[...] = x2

    xf = _rms(x2, gf_ref[...])
    _store_tok_tiles(xn_ref, xf)
    xh = xf.astype(BF16)
    xm = (xf - xh.astype(F32)).astype(BF16)
    wh = wrh_ref[...]
    logits = _dot(xh, wh) + (_dot(xh, wrm_ref[...]) + _dot(xm, wh)) + br_ref[...]

    lane = lax.broadcasted_iota(jnp.int32, logits.shape, 1).astype(F32)
    big = float(LANES)
    ninf = -jnp.inf
    glog = jnp.where(lane < N_GROUPS, logits, ninf)
    gmax = jnp.max(glog, axis=1, keepdims=True)
    gexp = jnp.exp(glog - gmax)
    gprob = gexp / jnp.sum(gexp, axis=1, keepdims=True)
    g_w = jnp.max(gprob, axis=1, keepdims=True)
    g_idx = jnp.min(jnp.where(gprob == g_w, lane, big), axis=1, keepdims=True)
    lo = N_GROUPS + EXPERTS_PER_GROUP * g_idx
    sel = jnp.logical_and(lane >= lo, lane < lo + EXPERTS_PER_GROUP)
    es = jnp.where(sel, logits, ninf)
    v1 = jnp.max(es, axis=1, keepdims=True)
    i1 = jnp.min(jnp.where(es == v1, lane, big), axis=1, keepdims=True)
    es2 = jnp.where(lane == i1, ninf, es)
    v2 = jnp.max(es2, axis=1, keepdims=True)
    i2 = jnp.min(jnp.where(es2 == v2, lane, big), axis=1, keepdims=True)
    t = jnp.exp(v2 - v1)
    w0 = g_w * (1.0 / (1.0 + t))
    w1 = g_w * (t / (1.0 + t))
    e0 = i1 - N_GROUPS
    e1 = i2 - N_GROUPS

    oh0 = lane == e0
    oh1 = lane == e1
    oh = jnp.where(jnp.logical_or(oh0, oh1), 1.0, 0.0)
    rr = lax.broadcasted_iota(jnp.int32, (tm, tm), 0)
    cc = lax.broadcasted_iota(jnp.int32, (tm, tm), 1)
    tri = jnp.where(cc < rr, 1.0, 0.0).astype(BF16)
    carry = carry_ref[...]
    before = _dot(tri, oh.astype(BF16)) + carry
    r0 = jnp.sum(jnp.where(oh0, before, 0.0), axis=1, keepdims=True)
    r1 = jnp.sum(jnp.where(oh1, before, 0.0), axis=1, keepdims=True)
    carry = carry + jnp.sum(oh, axis=0, keepdims=True)
    carry_ref[...] = carry
    cnt_ref[...] = jnp.broadcast_to(carry, cnt_ref.shape)

    slab = jnp.where(lane == 0, e0,
           jnp.where(lane == 1, e1,
           jnp.where(lane == 2, r0,
           jnp.where(lane == 3, r1,
           jnp.where(lane == 4, w0,
           jnp.where(lane == 5, w1, 0.0))))))
    slab_ref[...] = slab
    slabt_ref[...] = slab.T[0:8, :]


def _cross_router(x, gx, wq, q_gain, kmem, vmem, wo, gf, wr_hi, wr_mid, br, *, seq, mem_len, tm):
    T, D = x.shape
    tiles_per_seq = seq // tm
    row = lambda w: pl.BlockSpec((tm, w), lambda i: (i, 0))
    memspec = pl.BlockSpec((mem_len, D), lambda i: (i // tiles_per_seq, 0))
    return pl.pallas_call(
        _cross_router_kernel,
        grid=(T // tm,),
        in_specs=[row(D), _const_spec(gx.shape), _const_spec(wq.shape), _const_spec(q_gain.shape),
                  memspec, memspec, _const_spec(wo.shape), _const_spec(gf.shape),
                  _const_spec(wr_hi.shape), _const_spec(wr_mid.shape), _const_spec(br.shape)],
        out_specs=[row(D), pl.BlockSpec((tm * ROW_TILE, LANES), lambda i: (i, 0)), row(LANES),
                   pl.BlockSpec((8, tm), lambda i: (0, i)), _const_spec((8, LANES))],
        out_shape=[jax.ShapeDtypeStruct((T, D), F32), jax.ShapeDtypeStruct((T * ROW_TILE, LANES), F32),
                   jax.ShapeDtypeStruct((T, LANES), F32), jax.ShapeDtypeStruct((8, T), F32),
                   jax.ShapeDtypeStruct((8, LANES), F32)],
        scratch_shapes=[pltpu.VMEM((1, LANES), F32)],
        compiler_params=_cparams(("arbitrary",)),
        name="cross_attn_router",
    )(x, gx, wq, q_gain, kmem, vmem, wo, gf, wr_hi, wr_mid, br)


ROW_GROUP = 32


def _for_rows(n, fn):
    def body(g, carry):
        for u in range(ROW_GROUP):
            fn(g * ROW_GROUP + u)
        return carry
    lax.fori_loop(0, lax.shift_right_logical(n, ROW_GROUP.bit_length() - 1), body, 0)


def _expert_kernel(te_ref, nr_ref, nv_ref, tok_ref, tokn_ref, x_hbm, wg_ref, wu_ref, wd_ref, y_ref,
                   xbuf, sem, wgb, wub, wdb):
    t = pl.program_id(0)
    tm = y_ref.shape[0] // ROW_TILE
    n_valid = nv_ref[0]
    slot = t % 2

    def row_copy(src_row, dst_slot, r):
        return pltpu.make_async_copy(x_hbm.at[pl.ds(src_row * ROW_TILE, ROW_TILE)],
                                     xbuf.at[dst_slot, pl.ds(r * ROW_TILE, ROW_TILE)], sem.at[dst_slot])

    def issue(idx_ref, dst_slot, n):
        _for_rows(n, lambda r: row_copy(idx_ref[0, 0, r], dst_slot, r).start())

    @pl.when(t == 0)
    def _():
        xbuf[...] = jnp.zeros_like(xbuf)

    @pl.when(jnp.logical_and(t == 0, n_valid > 0))
    def _():
        issue(tok_ref, 0, nr_ref[0])

    @pl.when(t + 1 < n_valid)
    def _():
        issue(tokn_ref, 1 - slot, nr_ref[t + 1])

    @pl.when(jnp.logical_or(t == 0, te_ref[t] != te_ref[jnp.maximum(t - 1, 0)]))
    def _():
        wgb[...] = wg_ref[0, 0].astype(BF16)
        wub[...] = wu_ref[0, 0].astype(BF16)
        wdb[...] = wd_ref[0, 0].astype(BF16)

    @pl.when(t < n_valid)
    def _():
        _for_rows(nr_ref[t], lambda r: row_copy(0, slot, r).wait())
        x = jnp.concatenate([_load_tok_cols(xbuf, slot, 0, tm, j) for j in range(ROW_TILE)],
                            axis=1).astype(BF16)
        g = _dot(x, wgb[...])
        u = _dot(x, wub[...])
        h = (g * _sigmoid(g) * u).astype(BF16)
        _store_tok_tiles(y_ref, _dot(h, wdb[...]))

    @pl.when(t >= n_valid)
    def _():
        y_ref[...] = jnp.zeros_like(y_ref)


def _experts(tile_expert, tile_rows, n_valid, tok_sorted, xn3, w_gate, w_up, w_down, *, layer, tm):
    n_tiles = tile_expert.shape[0]
    D = ROW_TILE * LANES
    de = w_gate.shape[3]
    tok3 = tok_sorted.reshape(n_tiles, 1, tm)
    grid_spec = pltpu.PrefetchScalarGridSpec(
        num_scalar_prefetch=3,
        grid=(n_tiles,),
        in_specs=[
            pl.BlockSpec((1, 1, tm), lambda t, te, nr, nv: (t, 0, 0), memory_space=pltpu.SMEM),
            pl.BlockSpec((1, 1, tm), lambda t, te, nr, nv: (jnp.minimum(t + 1, n_tiles - 1), 0, 0),
                         memory_space=pltpu.SMEM),
            pl.BlockSpec(memory_space=pl.ANY),
            pl.BlockSpec((1, 1, D, de), lambda t, te, nr, nv: (layer, te[t], 0, 0)),
            pl.BlockSpec((1, 1, D, de), lambda t, te, nr, nv: (layer, te[t], 0, 0)),
            pl.BlockSpec((1, 1, de, D), lambda t, te, nr, nv: (layer, te[t], 0, 0)),
        ],
        out_specs=pl.BlockSpec((tm * ROW_TILE, LANES), lambda t, te, nr, nv: (t, 0)),
        scratch_shapes=[pltpu.VMEM((2, tm * ROW_TILE, LANES), F32), pltpu.SemaphoreType.DMA((2,)),
                        pltpu.VMEM((D, de), BF16), pltpu.VMEM((D, de), BF16), pltpu.VMEM((de, D), BF16)],
    )
    return pl.pallas_call(
        _expert_kernel,
        grid_spec=grid_spec,
        out_shape=jax.ShapeDtypeStruct((n_tiles * tm * ROW_TILE, LANES), F32),
        compiler_params=_cparams(("arbitrary",)),
        name="moe_experts",
    )(tile_expert, tile_rows, n_valid, tok3, tok3, xn3, w_gate, w_up, w_down)


def _invperm_kernel(lo_ref, hi_ref, pos_ref, tok_ref, *, n_tok):
    i = pl.program_id(0)
    chunk = pos_ref.shape[2]

    @pl.when(i == 0)
    def _():
        last = tok_ref.shape[0] - 1

        def per_range(e, carry):
            lo = lo_ref[e]

            def zero8(g, c):
                for u in range(ROW_TILE):
                    tok_ref[jnp.minimum(lo + g * ROW_TILE + u, last)] = 0
                return c
            n8 = lax.shift_right_logical(hi_ref[e] - lo + (ROW_TILE - 1), ROW_TILE.bit_length() - 1)
            lax.fori_loop(0, n8, zero8, 0)
            return carry
        lax.fori_loop(0, lo_ref.shape[0], per_range, 0)

    tbase = lax.rem(i * chunk, n_tok)

    def body(s, carry):
        tok_ref[pos_ref[0, 0, s]] = tbase + s
        return carry
    lax.fori_loop(0, chunk, body, 0, unroll=ROW_GROUP)


def _invperm(pad_lo, pad_hi, pos_flat, *, n_pos, n_tok, chunk):
    n = pos_flat.shape[0] // chunk
    grid_spec = pltpu.PrefetchScalarGridSpec(
        num_scalar_prefetch=2,
        grid=(n,),
        in_specs=[pl.BlockSpec((1, 1, chunk), lambda i, lo, hi: (i, 0, 0), memory_space=pltpu.SMEM)],
        out_specs=pl.BlockSpec(memory_space=pltpu.SMEM),
    )
    return pl.pallas_call(
        functools.partial(_invperm_kernel, n_tok=n_tok),
        grid_spec=grid_spec,
        out_shape=jax.ShapeDtypeStruct((n_pos,), jnp.int32),
        compiler_params=_cparams(("arbitrary",)),
        name="moe_invperm",
    )(pad_lo, pad_hi, pos_flat.reshape(n, 1, chunk))


def _combine_kernel(pos_ref, posn_ref, x_ref, slab_ref, y_hbm, o_ref, ybuf, sem):
    i = pl.program_id(0)
    n = pl.num_programs(0)
    tc = x_ref.shape[0]
    slot = i % 2

    def row_copy(src_row, dst_slot, r):
        return pltpu.make_async_copy(y_hbm.at[pl.ds(src_row * ROW_TILE, ROW_TILE)],
                                     ybuf.at[dst_slot, pl.ds(r * ROW_TILE, ROW_TILE)], sem.at[dst_slot])

    def issue(idx_ref, dst_slot):
        def body(r, carry):
            row_copy(idx_ref[0, 0, r], dst_slot, r).start()
            return carry
        lax.fori_loop(0, 2 * tc, body, 0, unroll=ROW_GROUP)

    @pl.when(i == 0)
    def _():
        issue(pos_ref, 0)

    @pl.when(i + 1 < n)
    def _():
        issue(posn_ref, 1 - slot)

    def wbody(r, carry):
        row_copy(0, slot, r).wait()
        return carry
    lax.fori_loop(0, 2 * tc, wbody, 0, unroll=ROW_GROUP)

    slab = slab_ref[...]
    w0 = slab[:, 4:5]
    w1 = slab[:, 5:6]
    w0 = jnp.broadcast_to(w0, (tc, LANES))
    w1 = jnp.broadcast_to(w1, (tc, LANES))
    for j in range(ROW_TILE):
        cols = slice(j * LANES, (j + 1) * LANES)
        o_ref[:, cols] = x_ref[:, cols] + (w0 * _load_tok_cols(ybuf, slot, 0, tc, j)
                                           + w1 * _load_tok_cols(ybuf, slot, tc, tc, j))


def _combine(pos3, x, slab, y, *, tc):
    T, D = x.shape
    n = T // tc
    return pl.pallas_call(
        _combine_kernel,
        grid=(n,),
        in_specs=[
            pl.BlockSpec((1, 1, 2 * tc), lambda i: (i, 0, 0), memory_space=pltpu.SMEM),
            pl.BlockSpec((1, 1, 2 * tc), lambda i: (jnp.minimum(i + 1, n - 1), 0, 0),
                         memory_space=pltpu.SMEM),
            pl.BlockSpec((tc, D), lambda i: (i, 0)),
            pl.BlockSpec((tc, LANES), lambda i: (i, 0)),
            pl.BlockSpec(memory_space=pl.ANY),
        ],
        out_specs=pl.BlockSpec((tc, D), lambda i: (i, 0)),
        out_shape=jax.ShapeDtypeStruct((T, D), F32),
        scratch_shapes=[pltpu.VMEM((2, 2 * tc * ROW_TILE, LANES), F32), pltpu.SemaphoreType.DMA((2,))],
        compiler_params=_cparams(("arbitrary",)),
        name="moe_combine",
    )(pos3, pos3, x, slab, y)


def _row(v):
    return v.reshape(1, -1).astype(F32)


def _pack_ab_weights(w_in, f_bias, q_gain, k_gain):
    D = w_in.shape[0]
    wq = w_in[:, 0:A_WIDTH].reshape(D, A_HEADS, HEAD_DIM)
    wk = w_in[:, A_WIDTH:2 * A_WIDTH].reshape(D, A_HEADS, HEAD_DIM)
    wqk = jnp.stack([wq, wk], axis=2).reshape(D, 2 * A_WIDTH)
    wv = w_in[:, 2 * A_WIDTH:3 * A_WIDTH]
    o = 3 * A_WIDTH
    wf = w_in[:, o:o + A_HEADS]
    wf3 = jnp.concatenate([wf, wf, wf, jnp.zeros((D, LANES - 3 * A_HEADS), F32)], axis=1)
    wxb = w_in[:, o + A_HEADS:o + A_HEADS + B_WIDTH]
    wgb = w_in[:, o + A_HEADS + B_WIDTH:o + A_HEADS + 2 * B_WIDTH]
    w_cat = jnp.concatenate([wqk, wv, wf3, wxb, wgb], axis=1).astype(BF16)
    fb3 = jnp.concatenate([f_bias, f_bias, f_bias, jnp.zeros((LANES - 3 * A_HEADS,), F32)]).reshape(1, LANES)
    gain_qk = jnp.concatenate([q_gain * (HEAD_DIM ** -0.5 * LOG2E), k_gain]).reshape(1, LANES)
    return w_cat, fb3, gain_qk


def _decay_placement():
    peq = [[0.0] * (A_HEADS * LANES) for _ in range(LANES)]
    pek = [[0.0] * (A_HEADS * LANES) for _ in range(LANES)]
    for h in range(A_HEADS):
        base = h * LANES + HEAD_DIM
        for part in range(3):
            peq[part * 8 + h][base + part] = 1.0
            peq[24][base + 3 + part] = 1.0
            pek[24][base + part] = 1.0
            pek[part * 8 + h][base + 3 + part] = -1.0
    gm = [[(1.0 / HEAD_DIM) if (r // HEAD_DIM) == (c // HEAD_DIM) else 0.0 for c in range(2 * LANES)]
          for r in range(2 * LANES)]
    return (jnp.array(peq, BF16), jnp.array(pek, BF16), jnp.array(gm, BF16))


def _block_diag(w):
    nb, bi, bo = w.shape
    eye = jnp.eye(nb, dtype=w.dtype)
    return (eye[:, None, :, None] * w[:, :, None, :]).reshape(nb * bi, nb * bo)


def _routing(slabt, counts, *, tm, n_tiles):
    e = slabt[0:2].astype(jnp.int32)
    r = slabt[2:4].astype(jnp.int32)
    cnt = counts[0, :N_EXPERTS].astype(jnp.int32)
    tiles_e = (cnt + tm - 1) // tm
    tile_end = jnp.cumsum(tiles_e)
    tile_start = tile_end - tiles_e
    hit = e[None] == jnp.arange(N_EXPERTS, dtype=jnp.int32)[:, None, None]
    pos = jnp.sum(jnp.where(hit, (tile_start * tm)[:, None, None], 0), axis=0) + r
    n_valid = tile_end[-1].astype(jnp.int32)
    tiles = jnp.arange(n_tiles, dtype=jnp.int32)
    te = jnp.sum((tiles[:, None] >= tile_end[None, :]).astype(jnp.int32), axis=1)
    te = jnp.minimum(te, N_EXPERTS - 1)
    rows = jnp.clip(jnp.take(cnt, te) - (tiles - jnp.take(tile_start, te)) * tm, 0, tm)
    valid = tiles < n_valid
    te = jnp.where(valid, te, jnp.take(te, jnp.maximum(n_valid - 1, 0)))
    rows = jnp.minimum(((rows + ROW_GROUP - 1) // ROW_GROUP) * ROW_GROUP, tm)
    rows = jnp.where(valid, rows, 0).astype(jnp.int32)
    end_all = jnp.full((1,), n_tiles * tm, jnp.int32)
    pad_lo = jnp.concatenate([tile_start * tm + cnt, tile_end[-1:] * tm]).astype(jnp.int32)
    pad_hi = jnp.concatenate([tile_end * tm, end_all]).astype(jnp.int32)
    return pos, te.astype(jnp.int32), rows, n_valid.reshape(1), pad_lo, pad_hi


def _moe(x2, xn3, slab, slabt, counts, w_gate, w_up, w_down, *, layer, tm_e, tc):
    T = x2.shape[0]
    n_tiles = (2 * T) // tm_e + N_EXPERTS
    pos, te, rows, n_valid, pad_lo, pad_hi = _routing(slabt, counts, tm=tm_e, n_tiles=n_tiles)
    tok_sorted = _invperm(pad_lo, pad_hi, pos.reshape(-1), n_pos=n_tiles * tm_e, n_tok=T,
                          chunk=min(8192, T))
    y3 = _experts(te, rows, n_valid, tok_sorted, xn3, w_gate, w_up, w_down, layer=layer, tm=tm_e)
    pos3 = pos.reshape(2, T // tc, tc).transpose(1, 0, 2).reshape(T // tc, 1, 2 * tc)
    return _combine(pos3, x2, slab, y3, tc=tc)


def _cross_and_moe(x, mem2, layer, norm_cross, norm_mem, norm_ffn, x_wq, x_wkv, x_wo, x_q_gain,
                   x_k_gain, moe_wg, moe_bg, moe_we, moe_be, moe_w_gate, moe_w_up, moe_w_down,
                   *, seq, mem_len, tm, tm_e, tc):
    kmem, vmem = _mem_kv(mem2, _row(norm_mem[layer]), x_wkv[layer].astype(BF16),
                         _row(x_k_gain[layer]), mem_len=mem_len)
    D = x.shape[1]
    wr = jnp.concatenate([moe_wg[layer], moe_we[layer],
                          jnp.zeros((D, LANES - N_GROUPS - N_EXPERTS), F32)], axis=1)
    wr_hi = wr.astype(BF16)
    wr_mid = (wr - wr_hi.astype(F32)).astype(BF16)
    br = jnp.concatenate([moe_bg[layer], moe_be[layer],
                          jnp.zeros((LANES - N_GROUPS - N_EXPERTS,), F32)]).reshape(1, LANES)
    x2, xn3, slab, slabt, counts = _cross_router(
        x, _row(norm_cross[layer]), x_wq[layer].astype(BF16), _row(x_q_gain[layer]), kmem, vmem,
        x_wo[layer].astype(BF16), _row(norm_ffn[layer]), wr_hi, wr_mid, br,
        seq=seq, mem_len=mem_len, tm=tm)
    return _moe(x2, xn3, slab, slabt, counts, moe_w_gate, moe_w_up, moe_w_down,
                layer=layer, tm_e=tm_e, tc=tc)


def kernel(x, mem, norm_mix, norm_cross, norm_mem, norm_ffn, ab_w_in, ab_f_bias, ab_q_gain, ab_k_gain, ab_conv_w, ab_conv_b, ab_wa, ab_ba, ab_wx, ab_bx, ab_lambda, ab_w_out, c_w_in, c_b_in, c_v_gain, c_w_s, c_b_s, c_w_out, x_wq, x_wkv, x_wo, x_q_gain, x_k_gain, moe_wg, moe_bg, moe_we, moe_be, moe_w_gate, moe_w_up, moe_w_down):
    B, S, D = x.shape
    mem_len = mem.shape[1]
    T = B * S
    tm = min(512, S)
    x2d = x.reshape(T, D)
    mem2 = mem.reshape(B * mem_len, D)
    common = dict(seq=S, mem_len=mem_len, tm=tm, tm_e=min(512, T), tc=min(256, T))
    tail = (norm_cross, norm_mem, norm_ffn, x_wq, x_wkv, x_wo, x_q_gain, x_k_gain,
            moe_wg, moe_bg, moe_we, moe_be, moe_w_gate, moe_w_up, moe_w_down)

    w_cat, fb3, gain_qk = _pack_ab_weights(ab_w_in[0], ab_f_bias[0], ab_q_gain[0], ab_k_gain[0])
    peq, pek, gmat = _decay_placement()
    qp, kp, v, xb, gb = _in_proj(x2d, _row(norm_mix[0]), w_cat, fb3, gain_qk, gmat, peq, pek,
                                 seq=S, tm=tm)
    attn = _fox_attention(qp, kp, v, batch=B, seq=S, tq=tm)
    w_gates = jnp.concatenate([_block_diag(ab_wa[0]), _block_diag(ab_wx[0])], axis=1).astype(BF16)
    b_gates = jnp.concatenate([ab_ba[0], ab_bx[0]]).reshape(1, -1)
    rec = _rglru(xb, gb, ab_conv_w[0], _row(ab_conv_b[0]), w_gates, b_gates, _row(ab_lambda[0]),
                 batch=B, seq=S, tt=min(256, S))
    x2d = _out_proj(x2d, attn, rec, ab_w_out[0].astype(BF16), tm=tm)
    x2d = _cross_and_moe(x2d, mem2, 0, *tail, **common)

    x2d = _gmlp(x2d, _row(norm_mix[1]), c_w_in[0].astype(BF16), _row(c_b_in[0]), _row(c_v_gain[0]),
                c_w_s[0], c_b_s[0].T, c_w_out[0].astype(BF16), tm=tm)
    x2d = _cross_and_moe(x2d, mem2, 1, *tail, **common)
    return x2d.reshape(B, S, D)
```

```python
import functools
import math

import jax
import jax.numpy as jnp
from jax import lax
from jax.experimental import pallas as pl
from jax.experimental.pallas import tpu as pltpu

EPS = 1e-6
A_HEADS = 8
HEAD_DIM = 64
A_WIDTH = A_HEADS * HEAD_DIM
B_WIDTH = 512
B_BLOCKS = 8
CONV_WIDTH = 4
RG_C = 8.0
C_GROUPS = 8
C_CHUNK = 128
X_HEADS = 4
N_GROUPS = 4
EXPERTS_PER_GROUP = 8
N_EXPERTS = N_GROUPS * EXPERTS_PER_GROUP

LOG2E = 1.4426950408889634
LANES = 128
VMEM_LIMIT = 56 * 1024 * 1024

F32 = jnp.float32
BF16 = jnp.bfloat16


def _cparams(sem, vmem=VMEM_LIMIT):
    return pltpu.CompilerParams(dimension_semantics=sem, vmem_limit_bytes=vmem)


def _const_spec(shape):
    nd = len(shape)
    return pl.BlockSpec(shape, lambda *_: (0,) * nd)


def _rms(x, g):
    ms = jnp.mean(x * x, axis=-1, keepdims=True)
    return x * lax.rsqrt(ms + EPS) * g


def _gelu(x):
    c = math.sqrt(2.0 / math.pi)
    return 0.5 * x * (1.0 + jnp.tanh(c * (x + 0.044715 * (x * x * x))))


def _sigmoid(x):
    return 1.0 / (1.0 + jnp.exp(-x))


def _softplus(x):
    return jnp.maximum(x, 0.0) + jnp.log1p(jnp.exp(-jnp.abs(x)))


def _dot(a, b):
    return jnp.dot(a, b, preferred_element_type=F32)


def _dot_nt(a, b):
    return lax.dot_general(a, b, (((1,), (1,)), ((), ())), preferred_element_type=F32)


def _split3(x):
    hi = x.astype(BF16).astype(F32)
    r = x - hi
    mid = r.astype(BF16).astype(F32)
    lo = (r - mid).astype(BF16).astype(F32)
    return hi, mid, lo


ROW_TILE = 8


def _store_tok_tiles(ref, val):
    m = val.shape[0]
    for j in range(ROW_TILE):
        ref[pl.ds(j, m, stride=ROW_TILE), :] = val[:, j * LANES:(j + 1) * LANES]


def _load_tok_cols(ref, slot, lo, n, j):
    return ref[slot, pl.ds(lo * ROW_TILE + j, n, stride=ROW_TILE), :]


def _in_proj_kernel(x_ref, g_ref, w_ref, fb_ref, gain_ref, gmat_ref, peq_ref, pek_ref,
                    qp_ref, kp_ref, ve_ref, vo_ref, xb_ref, gb_ref, carry_ref, *, tiles_per_seq):
    i = pl.program_id(0)
    tm = x_ref.shape[0]

    @pl.when(i % tiles_per_seq == 0)
    def _():
        carry_ref[...] = jnp.zeros_like(carry_ref)

    xn = _rms(x_ref[...], g_ref[...]).astype(BF16)
    proj = _dot(xn, w_ref[...])
    qk = proj[:, 0:1024]
    v = proj[:, 1024:1536]
    first_half = lax.broadcasted_iota(jnp.int32, v.shape, 1) % LANES < HEAD_DIM
    ve_ref[...] = jnp.where(first_half, v, 1.0).astype(BF16)
    vo_ref[...] = jnp.where(first_half, 1.0, v).astype(BF16)
    fl = proj[:, 1536:1664]
    xb_ref[...] = proj[:, 1664:2176]
    gb_ref[...] = proj[:, 2176:2688]

    logf = -_softplus(-(fl + fb_ref[...]))
    row = lax.broadcasted_iota(jnp.int32, logf.shape, 0)
    c = logf
    s = 1
    while s < tm:
        c = c + jnp.where(row >= s, pltpu.roll(c, s, axis=0), 0.0)
        s *= 2
    c = c + carry_ref[...]
    carry_ref[...] = c[tm - 1:tm, :]

    hi, mid, lo = _split3(c * LOG2E)
    lane = lax.broadcasted_iota(jnp.int32, c.shape, 1)
    cpack = jnp.where(lane < 8, hi, jnp.where(lane < 16, mid, jnp.where(lane < 24, lo,
                      jnp.where(lane == 24, 1.0, 0.0)))).astype(BF16)
    ext_q = _dot(cpack, peq_ref[...])
    ext_k = _dot(cpack, pek_ref[...])

    gmat = gmat_ref[...]
    gain = gain_ref[...]
    lane_b = lax.broadcasted_iota(jnp.int32, (tm, LANES), 1)
    for hp in range(A_HEADS // 2):
        pair = qk[:, hp * 2 * LANES:(hp + 1) * 2 * LANES]
        ms2 = _dot((pair * pair).astype(BF16), gmat)
        for hh in range(2):
            h = 2 * hp + hh
            blk = pair[:, hh * LANES:(hh + 1) * LANES]
            nb = blk * lax.rsqrt(ms2[:, hh * LANES:(hh + 1) * LANES] + EPS) * gain
            qp = jnp.where(lane_b < HEAD_DIM, nb, ext_q[:, h * LANES:(h + 1) * LANES])
            kp = jnp.where(lane_b < HEAD_DIM, pltpu.roll(nb, HEAD_DIM, axis=1),
                           ext_k[:, h * LANES:(h + 1) * LANES])
            qp_ref[:, h * LANES:(h + 1) * LANES] = qp.astype(BF16)
            kp_ref[:, h * LANES:(h + 1) * LANES] = kp.astype(BF16)


def _in_proj(x, g, w_cat, fb3, gain_qk, gmat, peq, pek, *, seq, tm):
    T, D = x.shape
    n = T // tm
    row = lambda w: pl.BlockSpec((tm, w), lambda i: (i, 0))
    return pl.pallas_call(
        functools.partial(_in_proj_kernel, tiles_per_seq=seq // tm),
        grid=(n,),
        in_specs=[row(D), _const_spec(g.shape), _const_spec(w_cat.shape), _const_spec(fb3.shape),
                  _const_spec(gain_qk.shape), _const_spec(gmat.shape), _const_spec(peq.shape),
                  _const_spec(pek.shape)],
        out_specs=[row(1024), row(1024), row(512), row(512), row(512), row(512)],
        out_shape=[jax.ShapeDtypeStruct((T, 1024), BF16), jax.ShapeDtypeStruct((T, 1024), BF16),
                   jax.ShapeDtypeStruct((T, 512), BF16), jax.ShapeDtypeStruct((T, 512), BF16),
                   jax.ShapeDtypeStruct((T, 512), F32), jax.ShapeDtypeStruct((T, 512), F32)],
        scratch_shapes=[pltpu.VMEM((1, LANES), F32)],
        compiler_params=_cparams(("arbitrary",)),
        name="in_proj_ab",
    )(x, g, w_cat, fb3, gain_qk, gmat, peq, pek)


def _fox_kernel(q_ref, k_ref, ve_ref, vo_ref, o_ref, m0, a0, m1, a1):
    i = pl.program_id(2)
    tq = q_ref.shape[0]
    stats = ((m0, a0, ve_ref), (m1, a1, vo_ref))
    for m_ref, a_ref, _ in stats:
        m_ref[...] = jnp.full_like(m_ref, -jnp.inf)
        a_ref[...] = jnp.zeros_like(a_ref)

    def tiles(js, mask_last):
        for hh in range(2):
            m_ref, a_ref, v_ref = stats[hh]
            q = q_ref[:, hh * LANES:(hh + 1) * LANES]
            ss = []
            for n, j in enumerate(js):
                rows = pl.ds(pl.multiple_of(j * tq, tq), tq)
                s = _dot_nt(q, k_ref[rows, hh * LANES:(hh + 1) * LANES])
                if mask_last and n == len(js) - 1:
                    r = lax.broadcasted_iota(jnp.int32, (tq, tq), 0)
                    c = lax.broadcasted_iota(jnp.int32, (tq, tq), 1)
                    s = jnp.where(c <= r, s, -jnp.inf)
                ss.append(s)
            s_max = ss[0]
            for s in ss[1:]:
                s_max = jnp.maximum(s_max, s)
            m_prev = m_ref[...]
            m_new = jnp.maximum(m_prev, jnp.max(s_max, axis=1, keepdims=True))
            m_wide = jnp.concatenate([m_new] * (tq // LANES), axis=1)
            acc = jnp.exp2(m_prev - m_new) * a_ref[...]
            for s, j in zip(ss, js):
                rows = pl.ds(pl.multiple_of(j * tq, tq), tq)
                acc = acc + _dot(jnp.exp2(s - m_wide).astype(BF16), v_ref[rows, :])
            a_ref[...] = acc
            m_ref[...] = m_new

    def body(jj, carry):
        tiles((2 * jj,), False)
        tiles((2 * jj + 1,), False)
        return carry
    lax.fori_loop(0, lax.shift_right_logical(i, 1), body, 0)

    @pl.when(i % 2 == 1)
    def _():
        tiles((i - 1,), False)
    tiles((i,), True)

    lane = lax.broadcasted_iota(jnp.int32, (tq, LANES), 1)
    acc0 = a0[...]
    acc1 = a1[...]
    o0 = acc0 / pltpu.roll(acc0, HEAD_DIM, axis=1)
    o1 = acc1 / pltpu.roll(acc1, HEAD_DIM, axis=1)
    o_ref[...] = jnp.where(lane < HEAD_DIM, o0, o1).astype(o_ref.dtype)


def _fox_attention(qp, kp, v_even, v_odd, *, batch, seq, tq):
    T = batch * seq
    nq = seq // tq
    stat = pltpu.VMEM((tq, LANES), F32)
    vspec = pl.BlockSpec((seq, LANES), lambda b, hp, i: (b, hp))
    return pl.pallas_call(
        _fox_kernel,
        grid=(batch, A_HEADS // 2, nq),
        in_specs=[
            pl.BlockSpec((tq, 2 * LANES), lambda b, hp, i: (b * nq + i, hp)),
            pl.BlockSpec((seq, 2 * LANES), lambda b, hp, i: (b, hp)),
            vspec, vspec,
        ],
        out_specs=pl.BlockSpec((tq, LANES), lambda b, hp, i: (b * nq + i, hp)),
        out_shape=jax.ShapeDtypeStruct((T, A_WIDTH), BF16),
        scratch_shapes=[stat] * 4,
        compiler_params=_cparams(("parallel", "parallel", "arbitrary")),
        name="fox_attention",
    )(qp, kp, v_even, v_odd)


def _rglru_kernel(xb_ref, gb_ref, cw_ref, cb_ref, wg_ref, bg_ref, lam_ref, o_ref,
                  xs_ref, h_ref):
    i = pl.program_id(1)
    tt = xb_ref.shape[0]
    pad = 8

    @pl.when(i == 0)
    def _():
        xs_ref[0:pad, :] = jnp.zeros((pad, B_WIDTH), F32)
        h_ref[...] = jnp.zeros_like(h_ref)

    xs_ref[pad:pad + tt, :] = xb_ref[...]
    xc = cb_ref[...]
    for k in range(CONV_WIDTH):
        off = pad - (CONV_WIDTH - 1) + k
        xc = xc + cw_ref[k:k + 1, :] * xs_ref[off:off + tt, :]
    xs_ref[0:pad, :] = xs_ref[tt:tt + pad, :]

    pre = _dot(xc.astype(BF16), wg_ref[...]) + bg_ref[...]
    r = _sigmoid(pre[:, 0:B_WIDTH])
    ig = _sigmoid(pre[:, B_WIDTH:2 * B_WIDTH])
    log_a = (-RG_C) * r * _softplus(-lam_ref[...])
    a = jnp.exp(log_a)
    th = jnp.tanh(log_a)
    beta = jnp.sqrt(-2.0 * th / (1.0 - th))
    bv = beta * (ig * xc)

    groups = tt // ROW_TILE
    a3 = a.reshape(groups, ROW_TILE, B_WIDTH)
    b3 = bv.reshape(groups, ROW_TILE, B_WIDTH)
    sub = lax.broadcasted_iota(jnp.int32, a3.shape, 1)
    s = 1
    while s < ROW_TILE:
        keep = sub >= s
        a_sh = jnp.where(keep, pltpu.roll(a3, s, axis=1), 1.0)
        b_sh = jnp.where(keep, pltpu.roll(b3, s, axis=1), 0.0)
        b3 = a3 * b_sh + b3
        a3 = a3 * a_sh
        s *= 2
    h_prev = h_ref[...]
    hs = []
    for g in range(groups):
        hg = b3[g] + a3[g] * h_prev
        hs.append(hg)
        h_prev = hg[ROW_TILE - 1:ROW_TILE, :]
    h_ref[...] = h_prev
    h = jnp.concatenate(hs, axis=0)
    o_ref[...] = (_gelu(gb_ref[...]) * h).astype(o_ref.dtype)


def _rglru(xb, gb, conv_w, conv_b, w_gates, b_gates, lam, *, batch, seq, tt):
    T = batch * seq
    nt = seq // tt
    row = pl.BlockSpec((tt, B_WIDTH), lambda b, i: (b * nt + i, 0))
    return pl.pallas_call(
        _rglru_kernel,
        grid=(batch, nt),
        in_specs=[row, row, _const_spec(conv_w.shape), _const_spec(conv_b.shape),
                  _const_spec(w_gates.shape), _const_spec(b_gates.shape), _const_spec(lam.shape)],
        out_specs=row,
        out_shape=jax.ShapeDtypeStruct((T, B_WIDTH), BF16),
        scratch_shapes=[pltpu.VMEM((tt + 8, B_WIDTH), F32), pltpu.VMEM((1, B_WIDTH), F32)],
        compiler_params=_cparams(("parallel", "arbitrary")),
        name="rglru",
    )(xb, gb, conv_w, conv_b, w_gates, b_gates, lam)


def _gmlp_kernel(x_ref, g_ref, wi_ref, bi_ref, vg_ref, ws_ref, bs_ref, wo_ref, o_ref):
    tm = x_ref.shape[0]
    cw = vg_ref.shape[1]
    gd = cw // C_GROUPS
    x = x_ref[...]
    xn = _rms(x, g_ref[...]).astype(BF16)

    u = _gelu(_dot(xn, wi_ref[:, 0:cw]) + bi_ref[:, 0:cw])
    v = _gelu(_dot(xn, wi_ref[:, cw:2 * cw]) + bi_ref[:, cw:2 * cw])
    vn = _rms(v, vg_ref[...]).astype(BF16)
    r = lax.broadcasted_iota(jnp.int32, (C_CHUNK, C_CHUNK), 0)
    c = lax.broadcasted_iota(jnp.int32, (C_CHUNK, C_CHUNK), 1)
    cols = []
    for g in range(C_GROUPS):
        wt = jnp.where(c <= r, ws_ref[g], 0.0).astype(BF16)
        bias = bs_ref[:, g:g + 1]
        mixed = jnp.concatenate(
            [_dot(wt, vn[ch * C_CHUNK:(ch + 1) * C_CHUNK, g * gd:(g + 1) * gd]) + bias
             for ch in range(tm // C_CHUNK)], axis=0)
        cols.append(u[:, g * gd:(g + 1) * gd] * mixed)
    y = jnp.concatenate(cols, axis=1).astype(BF16)
    o_ref[...] = x + _dot(y, wo_ref[...])


def _gmlp(x, g, w_in, b_in, v_gain, w_s, b_s_t, w_out, *, tm):
    T, D = x.shape
    return pl.pallas_call(
        _gmlp_kernel,
        grid=(T // tm,),
        in_specs=[pl.BlockSpec((tm, D), lambda i: (i, 0)), _const_spec(g.shape),
                  _const_spec(w_in.shape), _const_spec(b_in.shape), _const_spec(v_gain.shape),
                  _const_spec(w_s.shape), _const_spec(b_s_t.shape), _const_spec(w_out.shape)],
        out_specs=pl.BlockSpec((tm, D), lambda i: (i, 0)),
        out_shape=jax.ShapeDtypeStruct((T, D), F32),
        compiler_params=_cparams(("parallel",)),
        name="gmlp_mixer",
    )(x, g, w_in, b_in, v_gain, w_s, b_s_t, w_out)


def _mem_kv_kernel(m_ref, g_ref, w_ref, kg_ref, k_ref, v_ref):
    D = m_ref.shape[1]
    hd = D // X_HEADS
    mn = _rms(m_ref[...], g_ref[...]).astype(BF16)
    kv = _dot(mn, w_ref[...])
    v_ref[...] = kv[:, D:2 * D].astype(BF16)
    for h in range(X_HEADS):
        k_ref[:, h * hd:(h + 1) * hd] = _rms(kv[:, h * hd:(h + 1) * hd], kg_ref[...]).astype(BF16)


def _mem_kv(mem, g, wkv, k_gain, *, mem_len):
    M, D = mem.shape
    row = pl.BlockSpec((mem_len, D), lambda b: (b, 0))
    return pl.pallas_call(
        _mem_kv_kernel,
        grid=(M // mem_len,),
        in_specs=[row, _const_spec(g.shape), _const_spec(wkv.shape), _const_spec(k_gain.shape)],
        out_specs=[row, row],
        out_shape=[jax.ShapeDtypeStruct((M, D), BF16), jax.ShapeDtypeStruct((M, D), BF16)],
        compiler_params=_cparams(("parallel",)),
        name="mem_kv",
    )(mem, g, wkv, k_gain)


def _cross_router_kernel(*refs, mixer_out_proj):
    if mixer_out_proj:
        a_ref, r_ref, wab_ref = refs[:3]
        refs = refs[3:]
    (x_ref, gx_ref, wq_ref, qg_ref, k_ref, v_ref, wo_ref, gf_ref, wr_ref, br_ref,
     x2_ref, xn_ref, slab_ref, slabt_ref, cnt_ref, carry_ref) = refs
    i = pl.program_id(0)
    tm, D = x_ref.shape
    hd = D // X_HEADS

    @pl.when(i == 0)
    def _():
        carry_ref[...] = jnp.zeros_like(carry_ref)

    x = x_ref[...]
    if mixer_out_proj:
        x = (x + _dot(a_ref[...], wab_ref[0:A_WIDTH, :])
             + _dot(r_ref[...], wab_ref[A_WIDTH:A_WIDTH + B_WIDTH, :]))
    xn = _rms(x, gx_ref[...]).astype(BF16)
    q = _dot(xn, wq_ref[...])
    scale = hd ** -0.5
    outs = []
    for h in range(X_HEADS):
        qh = _rms(q[:, h * hd:(h + 1) * hd], qg_ref[...]) * scale
        s = _dot_nt(qh.astype(BF16), k_ref[:, h * hd:(h + 1) * hd])
        m = jnp.max(s, axis=1, keepdims=True)
        p = jnp.exp(s - m)
        p = p / jnp.sum(p, axis=1, keepdims=True)
        outs.append(_dot(p.astype(BF16), v_ref[:, h * hd:(h + 1) * hd]))
    o = jnp.concatenate(outs, axis=1).astype(BF16)
    x2 = x + _dot(o, wo_ref[...])
    x2_ref[...] = x2

    xf = _rms(x2, gf_ref[...])
    _store_tok_tiles(xn_ref, xf)
    xh = xf.astype(BF16)
    xm = (xf - xh.astype(F32)).astype(BF16)
    both = _dot(xh, wr_ref[...])
    logits = both[:, 0:LANES] + (both[:, LANES:2 * LANES] + _dot(xm, wr_ref[:, 0:LANES])) + br_ref[...]

    lane = lax.broadcasted_iota(jnp.int32, logits.shape, 1).astype(F32)
    big = float(LANES)
    ninf = -jnp.inf
    glog = jnp.where(lane < N_GROUPS, logits, ninf)
    gmax = jnp.max(glog, axis=1, keepdims=True)
    gexp = jnp.exp(glog - gmax)
    gprob = gexp / jnp.sum(gexp, axis=1, keepdims=True)
    g_w = jnp.max(gprob, axis=1, keepdims=True)
    g_idx = jnp.min(jnp.where(gprob == g_w, lane, big), axis=1, keepdims=True)
    lo = N_GROUPS + EXPERTS_PER_GROUP * g_idx
    sel = jnp.logical_and(lane >= lo, lane < lo + EXPERTS_PER_GROUP)
    es = jnp.where(sel, logits, ninf)
    v1 = jnp.max(es, axis=1, keepdims=True)
    i1 = jnp.min(jnp.where(es == v1, lane, big), axis=1, keepdims=True)
    es2 = jnp.where(lane == i1, ninf, es)
    v2 = jnp.max(es2, axis=1, keepdims=True)
    i2 = jnp.min(jnp.where(es2 == v2, lane, big), axis=1, keepdims=True)
    t = jnp.exp(v2 - v1)
    w0 = g_w * (1.0 / (1.0 + t))
    w1 = g_w * (t / (1.0 + t))
    e0 = i1 - N_GROUPS
    e1 = i2 - N_GROUPS

    oh0 = lane == e0
    oh1 = lane == e1
    oh = jnp.where(jnp.logical_or(oh0, oh1), 1.0, 0.0)
    rr = lax.broadcasted_iota(jnp.int32, (tm, tm), 0)
    cc = lax.broadcasted_iota(jnp.int32, (tm, tm), 1)
    tri = jnp.where(cc < rr, 1.0, 0.0).astype(BF16)
    carry = carry_ref[...]
    before = _dot(tri, oh.astype(BF16)) + carry
    r0 = jnp.sum(jnp.where(oh0, before, 0.0), axis=1, keepdims=True)
    r1 = jnp.sum(jnp.where(oh1, before, 0.0), axis=1, keepdims=True)
    carry = carry + jnp.sum(oh, axis=0, keepdims=True)
    carry_ref[...] = carry
    cnt_ref[...] = jnp.broadcast_to(carry, cnt_ref.shape)

    slab = jnp.where(lane == 0, e0,
           jnp.where(lane == 1, e1,
           jnp.where(lane == 2, r0,
           jnp.where(lane == 3, r1,
           jnp.where(lane == 4, w0,
           jnp.where(lane == 5, w1, 0.0))))))
    slab_ref[...] = slab
    slabt_ref[...] = slab.T[0:8, :]


def _cross_router(x, gx, wq, q_gain, kmem, vmem, wo, gf, wr_cat, br, *, seq, mem_len, tm,
                  mixer=None):
    T, D = x.shape
    tiles_per_seq = seq // tm
    row = lambda w: pl.BlockSpec((tm, w), lambda i: (i, 0))
    memspec = pl.BlockSpec((mem_len, D), lambda i: (i // tiles_per_seq, 0))
    mix_args, mix_specs = (), []
    if mixer is not None:
        mix_args = tuple(mixer)
        mix_specs = [row(A_WIDTH), row(B_WIDTH), _const_spec(mixer[2].shape)]
    return pl.pallas_call(
        functools.partial(_cross_router_kernel, mixer_out_proj=mixer is not None),
        grid=(T // tm,),
        in_specs=mix_specs + [
                  row(D), _const_spec(gx.shape), _const_spec(wq.shape), _const_spec(q_gain.shape),
                  memspec, memspec, _const_spec(wo.shape), _const_spec(gf.shape),
                  _const_spec(wr_cat.shape), _const_spec(br.shape)],
        out_specs=[row(D), pl.BlockSpec((tm * ROW_TILE, LANES), lambda i: (i, 0)), row(LANES),
                   pl.BlockSpec((8, tm), lambda i: (0, i)), _const_spec((8, LANES))],
        out_shape=[jax.ShapeDtypeStruct((T, D), F32), jax.ShapeDtypeStruct((T * ROW_TILE, LANES), F32),
                   jax.ShapeDtypeStruct((T, LANES), F32), jax.ShapeDtypeStruct((8, T), F32),
                   jax.ShapeDtypeStruct((8, LANES), F32)],
        scratch_shapes=[pltpu.VMEM((1, LANES), F32)],
        compiler_params=_cparams(("arbitrary",)),
        name="cross_attn_router",
    )(*mix_args, x, gx, wq, q_gain, kmem, vmem, wo, gf, wr_cat, br)


ROW_GROUP = 32


def _for_rows(n, fn):
    def body(g, carry):
        for u in range(ROW_GROUP):
            fn(g * ROW_GROUP + u, u)
        return carry
    groups = n // ROW_GROUP if isinstance(n, int) else lax.shift_right_logical(n, ROW_GROUP.bit_length() - 1)
    lax.fori_loop(0, groups, body, 0)


def _expert_kernel(te_ref, nr_ref, nv_ref, tok_ref, tokn_ref, x_hbm, wg_ref, wu_ref, wd_ref, y_ref,
                   xbuf, sem, wgb, wub, wdb):
    t = pl.program_id(0)
    tm = y_ref.shape[0] // ROW_TILE
    n_valid = nv_ref[0]
    slot = t % 2

    def row_copy(src_row, dst_slot, r):
        return pltpu.make_async_copy(x_hbm.at[pl.ds(src_row * ROW_TILE, ROW_TILE)],
                                     xbuf.at[dst_slot, pl.ds(r * ROW_TILE, ROW_TILE)], sem.at[dst_slot])

    def issue(idx_ref, dst_slot, n):
        _for_rows(n, lambda r, u: row_copy(idx_ref[0, 0, r], dst_slot, r).start())

    @pl.when(t == 0)
    def _():
        xbuf[...] = jnp.zeros_like(xbuf)

    @pl.when(jnp.logical_and(t == 0, n_valid > 0))
    def _():
        issue(tok_ref, 0, nr_ref[0])

    @pl.when(t + 1 < n_valid)
    def _():
        issue(tokn_ref, 1 - slot, nr_ref[t + 1])

    @pl.when(jnp.logical_or(t == 0, te_ref[t] != te_ref[jnp.maximum(t - 1, 0)]))
    def _():
        wgb[...] = wg_ref[0, 0].astype(BF16)
        wub[...] = wu_ref[0, 0].astype(BF16)
        wdb[...] = wd_ref[0, 0].astype(BF16)

    @pl.when(t < n_valid)
    def _():
        _for_rows(nr_ref[t], lambda r, u: row_copy(0, slot, r).wait())
        x = jnp.concatenate([_load_tok_cols(xbuf, slot, 0, tm, j) for j in range(ROW_TILE)],
                            axis=1).astype(BF16)
        g = _dot(x, wgb[...])
        u = _dot(x, wub[...])
        h = (g * _sigmoid(g) * u).astype(BF16)
        _store_tok_tiles(y_ref, _dot(h, wdb[...]))

    @pl.when(t >= n_valid)
    def _():
        y_ref[...] = jnp.zeros_like(y_ref)


def _experts(tile_expert, tile_rows, n_valid, tok_sorted, xn3, w_gate, w_up, w_down, *, layer, tm):
    n_tiles = tile_expert.shape[0]
    D = ROW_TILE * LANES
    de = w_gate.shape[3]
    tok3 = tok_sorted.reshape(n_tiles, 1, tm)
    grid_spec = pltpu.PrefetchScalarGridSpec(
        num_scalar_prefetch=3,
        grid=(n_tiles,),
        in_specs=[
            pl.BlockSpec((1, 1, tm), lambda t, te, nr, nv: (t, 0, 0), memory_space=pltpu.SMEM),
            pl.BlockSpec((1, 1, tm), lambda t, te, nr, nv: (jnp.minimum(t + 1, n_tiles - 1), 0, 0),
                         memory_space=pltpu.SMEM),
            pl.BlockSpec(memory_space=pl.ANY),
            pl.BlockSpec((1, 1, D, de), lambda t, te, nr, nv: (layer, te[t], 0, 0)),
            pl.BlockSpec((1, 1, D, de), lambda t, te, nr, nv: (layer, te[t], 0, 0)),
            pl.BlockSpec((1, 1, de, D), lambda t, te, nr, nv: (layer, te[t], 0, 0)),
        ],
        out_specs=pl.BlockSpec((tm * ROW_TILE, LANES), lambda t, te, nr, nv: (t, 0)),
        scratch_shapes=[pltpu.VMEM((2, tm * ROW_TILE, LANES), F32), pltpu.SemaphoreType.DMA((2,)),
                        pltpu.VMEM((D, de), BF16), pltpu.VMEM((D, de), BF16), pltpu.VMEM((de, D), BF16)],
    )
    return pl.pallas_call(
        _expert_kernel,
        grid_spec=grid_spec,
        out_shape=jax.ShapeDtypeStruct((n_tiles * tm * ROW_TILE, LANES), F32),
        compiler_params=_cparams(("arbitrary",)),
        name="moe_experts",
    )(tile_expert, tile_rows, n_valid, tok3, tok3, xn3, w_gate, w_up, w_down)


def _invperm_kernel(lo_ref, hi_ref, pos_ref, tok_ref, *, n_tok):
    i = pl.program_id(0)
    chunk = pos_ref.shape[2]

    @pl.when(i == 0)
    def _():
        last = tok_ref.shape[0] - 1

        def per_range(e, carry):
            lo = lo_ref[e]

            def zero8(g, c):
                for u in range(ROW_TILE):
                    tok_ref[jnp.minimum(lo + g * ROW_TILE + u, last)] = 0
                return c
            n8 = lax.shift_right_logical(hi_ref[e] - lo + (ROW_TILE - 1), ROW_TILE.bit_length() - 1)
            lax.fori_loop(0, n8, zero8, 0)
            return carry
        lax.fori_loop(0, lo_ref.shape[0], per_range, 0)

    tbase = lax.rem(i * chunk, n_tok)

    def body(s, carry):
        tok_ref[pos_ref[0, 0, s]] = tbase + s
        return carry
    lax.fori_loop(0, chunk, body, 0, unroll=ROW_GROUP)


def _invperm(pad_lo, pad_hi, pos_flat, *, n_pos, n_tok, chunk):
    n = pos_flat.shape[0] // chunk
    grid_spec = pltpu.PrefetchScalarGridSpec(
        num_scalar_prefetch=2,
        grid=(n,),
        in_specs=[pl.BlockSpec((1, 1, chunk), lambda i, lo, hi: (i, 0, 0), memory_space=pltpu.SMEM)],
        out_specs=pl.BlockSpec(memory_space=pltpu.SMEM),
    )
    return pl.pallas_call(
        functools.partial(_invperm_kernel, n_tok=n_tok),
        grid_spec=grid_spec,
        out_shape=jax.ShapeDtypeStruct((n_pos,), jnp.int32),
        compiler_params=_cparams(("arbitrary",)),
        name="moe_invperm",
    )(pad_lo, pad_hi, pos_flat.reshape(n, 1, chunk))


def _combine_kernel(pos_ref, posn_ref, x_ref, slab_ref, y_hbm, o_ref, ybuf, sem):
    i = pl.program_id(0)
    n = pl.num_programs(0)
    tc = x_ref.shape[0]
    slot = i % 2

    def row_copy(src_row, dst_slot, r):
        return pltpu.make_async_copy(y_hbm.at[pl.ds(src_row * ROW_TILE, ROW_TILE)],
                                     ybuf.at[dst_slot, pl.ds(r * ROW_TILE, ROW_TILE)], sem.at[dst_slot])

    def issue(idx_ref, dst_slot):
        _for_rows(2 * tc, lambda r, u: row_copy(idx_ref[0, 0, r], dst_slot, r).start(priority=u % 2))

    @pl.when(i == 0)
    def _():
        issue(pos_ref, 0)

    @pl.when(i + 1 < n)
    def _():
        issue(posn_ref, 1 - slot)

    _for_rows(2 * tc, lambda r, u: row_copy(0, slot, r).wait())

    slab = slab_ref[...]
    w0 = slab[:, 4:5]
    w1 = slab[:, 5:6]
    w0 = jnp.broadcast_to(w0, (tc, LANES))
    w1 = jnp.broadcast_to(w1, (tc, LANES))
    for j in range(ROW_TILE):
        cols = slice(j * LANES, (j + 1) * LANES)
        o_ref[:, cols] = x_ref[:, cols] + (w0 * _load_tok_cols(ybuf, slot, 0, tc, j)
                                           + w1 * _load_tok_cols(ybuf, slot, tc, tc, j))


def _combine(pos3, x, slab, y, *, tc):
    T, D = x.shape
    n = T // tc
    return pl.pallas_call(
        _combine_kernel,
        grid=(n,),
        in_specs=[
            pl.BlockSpec((1, 1, 2 * tc), lambda i: (i, 0, 0), memory_space=pltpu.SMEM),
            pl.BlockSpec((1, 1, 2 * tc), lambda i: (jnp.minimum(i + 1, n - 1), 0, 0),
                         memory_space=pltpu.SMEM),
            pl.BlockSpec((tc, D), lambda i: (i, 0)),
            pl.BlockSpec((tc, LANES), lambda i: (i, 0)),
            pl.BlockSpec(memory_space=pl.ANY),
        ],
        out_specs=pl.BlockSpec((tc, D), lambda i: (i, 0)),
        out_shape=jax.ShapeDtypeStruct((T, D), F32),
        scratch_shapes=[pltpu.VMEM((2, 2 * tc * ROW_TILE, LANES), F32), pltpu.SemaphoreType.DMA((2,))],
        compiler_params=_cparams(("arbitrary",)),
        name="moe_combine",
    )(pos3, pos3, x, slab, y)


def _row(v):
    return v.reshape(1, -1).astype(F32)


def _pack_ab_weights(w_in, f_bias, q_gain, k_gain):
    D = w_in.shape[0]
    wq = w_in[:, 0:A_WIDTH].reshape(D, A_HEADS, HEAD_DIM)
    wk = w_in[:, A_WIDTH:2 * A_WIDTH].reshape(D, A_HEADS, HEAD_DIM)
    wqk = jnp.stack([wq, wk], axis=2).reshape(D, 2 * A_WIDTH)
    wv = w_in[:, 2 * A_WIDTH:3 * A_WIDTH]
    o = 3 * A_WIDTH
    wf = w_in[:, o:o + A_HEADS]
    wf3 = jnp.concatenate([wf, wf, wf, jnp.zeros((D, LANES - 3 * A_HEADS), F32)], axis=1)
    wxb = w_in[:, o + A_HEADS:o + A_HEADS + B_WIDTH]
    wgb = w_in[:, o + A_HEADS + B_WIDTH:o + A_HEADS + 2 * B_WIDTH]
    w_cat = jnp.concatenate([wqk, wv, wf3, wxb, wgb], axis=1).astype(BF16)
    fb3 = jnp.concatenate([f_bias, f_bias, f_bias, jnp.zeros((LANES - 3 * A_HEADS,), F32)]).reshape(1, LANES)
    gain_qk = jnp.concatenate([q_gain * (HEAD_DIM ** -0.5 * LOG2E), k_gain]).reshape(1, LANES)
    return w_cat, fb3, gain_qk


def _decay_placement():
    peq = [[0.0] * (A_HEADS * LANES) for _ in range(LANES)]
    pek = [[0.0] * (A_HEADS * LANES) for _ in range(LANES)]
    for h in range(A_HEADS):
        base = h * LANES + HEAD_DIM
        for part in range(3):
            peq[part * 8 + h][base + part] = 1.0
            peq[24][base + 3 + part] = 1.0
            pek[24][base + part] = 1.0
            pek[part * 8 + h][base + 3 + part] = -1.0
    gm = [[(1.0 / HEAD_DIM) if (r // HEAD_DIM) == (c // HEAD_DIM) else 0.0 for c in range(2 * LANES)]
          for r in range(2 * LANES)]
    return (jnp.array(peq, BF16), jnp.array(pek, BF16), jnp.array(gm, BF16))


def _block_diag(w):
    nb, bi, bo = w.shape
    eye = jnp.eye(nb, dtype=w.dtype)
    return (eye[:, None, :, None] * w[:, :, None, :]).reshape(nb * bi, nb * bo)


def _routing(slabt, counts, *, tm, n_tiles):
    e = slabt[0:2].astype(jnp.int32)
    r = slabt[2:4].astype(jnp.int32)
    cnt = counts[0, :N_EXPERTS].astype(jnp.int32)
    tiles_e = (cnt + tm - 1) // tm
    tile_end = jnp.cumsum(tiles_e)
    tile_start = tile_end - tiles_e
    hit = e[None] == jnp.arange(N_EXPERTS, dtype=jnp.int32)[:, None, None]
    pos = jnp.sum(jnp.where(hit, (tile_start * tm)[:, None, None], 0), axis=0) + r
    n_valid = tile_end[-1].astype(jnp.int32)
    tiles = jnp.arange(n_tiles, dtype=jnp.int32)
    te = jnp.sum((tiles[:, None] >= tile_end[None, :]).astype(jnp.int32), axis=1)
    te = jnp.minimum(te, N_EXPERTS - 1)
    rows = jnp.clip(jnp.take(cnt, te) - (tiles - jnp.take(tile_start, te)) * tm, 0, tm)
    valid = tiles < n_valid
    te = jnp.where(valid, te, jnp.take(te, jnp.maximum(n_valid - 1, 0)))
    rows = jnp.minimum(((rows + ROW_GROUP - 1) // ROW_GROUP) * ROW_GROUP, tm)
    rows = jnp.where(valid, rows, 0).astype(jnp.int32)
    end_all = jnp.full((1,), n_tiles * tm, jnp.int32)
    pad_lo = jnp.concatenate([tile_start * tm + cnt, tile_end[-1:] * tm]).astype(jnp.int32)
    pad_hi = jnp.concatenate([tile_end * tm, end_all]).astype(jnp.int32)
    return pos, te.astype(jnp.int32), rows, n_valid.reshape(1), pad_lo, pad_hi


def _moe(x2, xn3, slab, slabt, counts, w_gate, w_up, w_down, *, layer, tm_e, tc):
    T = x2.shape[0]
    n_tiles = (2 * T) // tm_e + N_EXPERTS
    pos, te, rows, n_valid, pad_lo, pad_hi = _routing(slabt, counts, tm=tm_e, n_tiles=n_tiles)
    tok_sorted = _invperm(pad_lo, pad_hi, pos.reshape(-1), n_pos=n_tiles * tm_e, n_tok=T,
                          chunk=min(8192, T))
    y3 = _experts(te, rows, n_valid, tok_sorted, xn3, w_gate, w_up, w_down, layer=layer, tm=tm_e)
    pos3 = pos.reshape(2, T // tc, tc).transpose(1, 0, 2).reshape(T // tc, 1, 2 * tc)
    return _combine(pos3, x2, slab, y3, tc=tc)


def _cross_and_moe(x, mem2, layer, norm_cross, norm_mem, norm_ffn, x_wq, x_wkv, x_wo, x_q_gain,
                   x_k_gain, moe_wg, moe_bg, moe_we, moe_be, moe_w_gate, moe_w_up, moe_w_down,
                   *, seq, mem_len, tm, tm_e, tc, mixer=None):
    kmem, vmem = _mem_kv(mem2, _row(norm_mem[layer]), x_wkv[layer].astype(BF16),
                         _row(x_k_gain[layer]), mem_len=mem_len)
    D = x.shape[1]
    wr = jnp.concatenate([moe_wg[layer], moe_we[layer],
                          jnp.zeros((D, LANES - N_GROUPS - N_EXPERTS), F32)], axis=1)
    wr_hi = wr.astype(BF16)
    wr_cat = jnp.concatenate([wr_hi, (wr - wr_hi.astype(F32)).astype(BF16)], axis=1)
    br = jnp.concatenate([moe_bg[layer], moe_be[layer],
                          jnp.zeros((LANES - N_GROUPS - N_EXPERTS,), F32)]).reshape(1, LANES)
    x2, xn3, slab, slabt, counts = _cross_router(
        x, _row(norm_cross[layer]), x_wq[layer].astype(BF16), _row(x_q_gain[layer]), kmem, vmem,
        x_wo[layer].astype(BF16), _row(norm_ffn[layer]), wr_cat, br,
        seq=seq, mem_len=mem_len, tm=tm, mixer=mixer)
    return _moe(x2, xn3, slab, slabt, counts, moe_w_gate, moe_w_up, moe_w_down,
                layer=layer, tm_e=tm_e, tc=tc)


def kernel(x, mem, norm_mix, norm_cross, norm_mem, norm_ffn, ab_w_in, ab_f_bias, ab_q_gain, ab_k_gain, ab_conv_w, ab_conv_b, ab_wa, ab_ba, ab_wx, ab_bx, ab_lambda, ab_w_out, c_w_in, c_b_in, c_v_gain, c_w_s, c_b_s, c_w_out, x_wq, x_wkv, x_wo, x_q_gain, x_k_gain, moe_wg, moe_bg, moe_we, moe_be, moe_w_gate, moe_w_up, moe_w_down):
    B, S, D = x.shape
    mem_len = mem.shape[1]
    T = B * S
    tm = min(512, S)
    x2d = x.reshape(T, D)
    mem2 = mem.reshape(B * mem_len, D)
    common = dict(seq=S, mem_len=mem_len, tm=tm, tm_e=min(512, T), tc=min(256, T))
    tail = (norm_cross, norm_mem, norm_ffn, x_wq, x_wkv, x_wo, x_q_gain, x_k_gain,
            moe_wg, moe_bg, moe_we, moe_be, moe_w_gate, moe_w_up, moe_w_down)

    w_cat, fb3, gain_qk = _pack_ab_weights(ab_w_in[0], ab_f_bias[0], ab_q_gain[0], ab_k_gain[0])
    peq, pek, gmat = _decay_placement()
    qp, kp, ve, vo, xb, gb = _in_proj(x2d, _row(norm_mix[0]), w_cat, fb3, gain_qk, gmat, peq, pek,
                                      seq=S, tm=tm)
    attn = _fox_attention(qp, kp, ve, vo, batch=B, seq=S, tq=tm)
    w_gates = jnp.concatenate([_block_diag(ab_wa[0]), _block_diag(ab_wx[0])], axis=1).astype(BF16)
    b_gates = jnp.concatenate([ab_ba[0], ab_bx[0]]).reshape(1, -1)
    rec = _rglru(xb, gb, ab_conv_w[0], _row(ab_conv_b[0]), w_gates, b_gates, _row(ab_lambda[0]),
                 batch=B, seq=S, tt=min(256, S))
    x2d = _cross_and_moe(x2d, mem2, 0, *tail, **common, mixer=(attn, rec, ab_w_out[0].astype(BF16)))

    x2d = _gmlp(x2d, _row(norm_mix[1]), c_w_in[0].astype(BF16), _row(c_b_in[0]), _row(c_v_gain[0]),
                c_w_s[0], c_b_s[0].T, c_w_out[0].astype(BF16), tm=tm)
    x2d = _cross_and_moe(x2d, mem2, 1, *tail, **common)
    return x2d.reshape(B, S, D)
```
